```python
import math
import jax, jax.numpy as jnp
from jax import lax
import numpy as np

D_MODEL = 1024
BATCH = 2
SEQ = 8192
DEPTH = 4

CTX_LEN = 256
GRID_W = 64
ROPE_DIM = 32
ROPE_BASE = 10000.0
DA_HEADS = 6
DA_QK = ROPE_DIM
DA_V = 2 * DA_QK
NA_HEADS = 6
NA_DIM = 64
NA_KH = 8
NA_KW = 16
MLA_HEADS = 4
MLA_Q_RANK = 256
MLA_KV_RANK = 128
MLA_NOPE = 64
MLA_ROPE = ROPE_DIM
MLA_V = 64
MIX_WIDTH = DA_HEADS * DA_V + NA_HEADS * NA_DIM + MLA_HEADS * MLA_V
SPLITS = (DA_HEADS * 2 * DA_QK, DA_HEADS * 2 * DA_QK, DA_HEADS * DA_V,
          NA_HEADS * NA_DIM, NA_HEADS * NA_DIM, NA_HEADS * NA_DIM,
          MLA_Q_RANK, MLA_KV_RANK, MLA_ROPE)
W_IN_COLS = sum(SPLITS)
SPLIT_POINTS = tuple(int(v) for v in np.cumsum(SPLITS)[:-1])
DA_SCALE = DA_QK ** -0.5
NA_SCALE = NA_DIM ** -0.5
MLA_SCALE = (MLA_NOPE + MLA_ROPE) ** -0.5
D_FF = 2816
CONV_W = 3
Q_BLOCK = 128
LN_EPS = 1e-6
NEG_INF = -1e30
DEEPNORM_ALPHA = (2 * DEPTH) ** 0.25
DEEPNORM_BETA = (8 * DEPTH) ** -0.25

kernel_name = "hybrid_diffmla_natten_convffn_dit"


def _layernorm(x):
    xf = x.astype(jnp.float32)
    mu = jnp.mean(xf, axis=-1, keepdims=True)
    var = jnp.mean(jnp.square(xf - mu), axis=-1, keepdims=True)
    return ((xf - mu) * lax.rsqrt(var + LN_EPS)).astype(x.dtype)


def _rmsnorm(x, g):
    xf = x.astype(jnp.float32)
    y = xf * lax.rsqrt(jnp.mean(jnp.square(xf), axis=-1, keepdims=True) + LN_EPS)
    return y.astype(x.dtype) * g


def _axial_tables(n, dtype):
    t = jnp.arange(n)
    row = (t // GRID_W).astype(jnp.float32)
    col = (t % GRID_W).astype(jnp.float32)
    axis_dim = ROPE_DIM // 2
    inv_freq = ROPE_BASE ** (-jnp.arange(0, axis_dim, 2, dtype=jnp.float32) / axis_dim)

    def cs(pos):
        ang = pos[:, None] * inv_freq[None, :]
        ang = jnp.concatenate([ang, ang], axis=-1)
        return jnp.cos(ang).astype(dtype), jnp.sin(ang).astype(dtype)

    cr, sr = cs(row)
    cc, sc = cs(col)
    return (cr, sr, cc, sc)


def _rope_half(x, cos, sin):
    h = x.shape[-1] // 2
    rot = jnp.concatenate([-x[..., h:], x[..., :h]], axis=-1)
    return x * cos + rot * sin


def _rope_2d(x, tabs):
    cr, sr, cc, sc = tabs
    h = x.shape[-1] // 2
    return jnp.concatenate([_rope_half(x[..., :h], cr, sr), _rope_half(x[..., h:], cc, sc)], axis=-1)


def _split_heads(proj, tabs, q_norm_w, kv_norm_w, w_uq, w_ukv):
    a_q, a_k, a_v, n_q, n_k, n_v, c_q, c_kv, k_r = jnp.split(proj, SPLIT_POINTS, axis=-1)
    b, n, _ = proj.shape
    a_q = a_q.reshape(b, n, DA_HEADS, 2, DA_QK).transpose(0, 2, 1, 3, 4)
    a_k = a_k.reshape(b, n, DA_HEADS, 2, DA_QK).transpose(0, 2, 1, 3, 4)
    a_v = a_v.reshape(b, n, DA_HEADS, DA_V).transpose(0, 2, 1, 3)
    n_q = n_q.reshape(b, n, NA_HEADS, NA_DIM).transpose(0, 2, 1, 3)
    n_k = n_k.reshape(b, n, NA_HEADS, NA_DIM).transpose(0, 2, 1, 3)
    n_v = n_v.reshape(b, n, NA_HEADS, NA_DIM).transpose(0, 2, 1, 3)
    q_c = (_rmsnorm(c_q, q_norm_w) @ w_uq).reshape(b, n, MLA_HEADS, MLA_NOPE + MLA_ROPE).transpose(0, 2, 1, 3)
    kv_c = (_rmsnorm(c_kv, kv_norm_w) @ w_ukv).reshape(b, n, MLA_HEADS, MLA_NOPE + MLA_V).transpose(0, 2, 1, 3)
    m_qn, m_qr = q_c[..., :MLA_NOPE], q_c[..., MLA_NOPE:]
    m_kn, m_v = kv_c[..., :MLA_NOPE], kv_c[..., MLA_NOPE:]
    k_r = k_r[:, None]
    if tabs is not None:
        tabs_da = tuple(t[:, None, :] for t in tabs)
        a_q = _rope_2d(a_q, tabs_da)
        a_k = _rope_2d(a_k, tabs_da)
        m_qr = _rope_2d(m_qr, tabs)
        k_r = _rope_2d(k_r, tabs)
    m_q = jnp.concatenate([m_qn, m_qr], axis=-1)
    m_k = jnp.concatenate([m_kn, jnp.broadcast_to(k_r, m_kn.shape[:-1] + (MLA_ROPE,))], axis=-1)
    return (a_q, a_k, a_v, n_q, n_k, n_v, m_q, m_k, m_v)


def _diff_attend(q, k, v, lam, scale):
    s = jnp.einsum('bhqcd,bhkcd->bhcqk', q, k).astype(jnp.float32) * scale
    p = jax.nn.softmax(s, axis=-1)
    p = p[:, :, 0] - lam * p[:, :, 1]
    return jnp.einsum('bhqk,bhkd->bhqd', p.astype(v.dtype), v)


def _softmax_attend(q, k, v, scale):
    s = jnp.einsum('bhqd,bhkd->bhqk', q, k).astype(jnp.float32) * scale
    p = jax.nn.softmax(s, axis=-1)
    return jnp.einsum('bhqk,bhkd->bhqd', p.astype(v.dtype), v)


def _sweep_query_blocks(fn, q):
    b, h, n = q.shape[:3]
    nb = n // Q_BLOCK
    qb = jnp.moveaxis(q.reshape((b, h, nb, Q_BLOCK) + q.shape[3:]), 2, 0)
    out = lax.map(fn, qb)
    return jnp.moveaxis(out, 0, 2).reshape(b, h, n, out.shape[-1])


def _neighbourhood_attend(q, k, v, k_ctx, v_ctx, rpb, scale):
    b, h, n, d = q.shape
    rows = n // GRID_W
    kh = min(NA_KH, rows)
    qg = q.reshape(b, h, rows, GRID_W, d)
    kg = k.reshape(b, h, rows, GRID_W, d)
    vg = v.reshape(b, h, rows, GRID_W, d)
    r_idx = jnp.arange(rows)
    r0 = jnp.clip(r_idx - kh // 2, 0, rows - kh)
    key_rows = r0[:, None] + jnp.arange(kh)[None, :]
    k_win = kg[:, :, key_rows]
    v_win = vg[:, :, key_rows]
    c_idx = jnp.arange(GRID_W)
    c0 = jnp.clip(c_idx - NA_KW // 2, 0, GRID_W - NA_KW)
    in_band = (c_idx[None, :] >= c0[:, None]) & (c_idx[None, :] < c0[:, None] + NA_KW)
    dr = key_rows - r_idx[:, None] + (NA_KH - 1)
    dc = jnp.clip(c_idx[None, :] - c_idx[:, None], -(NA_KW - 1), NA_KW - 1) + (NA_KW - 1)
    bias = rpb[:, dr[:, None, :, None], dc[None, :, None, :]]
    s_win = jnp.einsum('bhrqd,bhrjkd->bhrqjk', qg, k_win).astype(jnp.float32) * scale
    s_win = s_win + bias[None].astype(jnp.float32)
    s_win = jnp.where(in_band[:, None, :], s_win, NEG_INF)
    s_ctx = jnp.einsum('bhrqd,bhcd->bhrqc', qg, k_ctx).astype(jnp.float32) * scale
    nw = kh * GRID_W
    s = jnp.concatenate([s_win.reshape(b, h, rows, GRID_W, nw), s_ctx], axis=-1)
    p = jax.nn.softmax(s, axis=-1).astype(v.dtype)
    p_win = p[..., :nw].reshape(b, h, rows, GRID_W, kh, GRID_W)
    p_ctx = p[..., nw:]
    out = (jnp.einsum('bhrqjk,bhrjkd->bhrqd', p_win, v_win)
           + jnp.einsum('bhrqc,bhcd->bhrqd', p_ctx, v_ctx))
    return out.reshape(b, h, n, d)


def _merge(o):
    b, h, n, d = o.shape
    return o.transpose(0, 2, 1, 3).reshape(b, n, h * d)


def _mix_out(o_a, o_b, o_c, diff_norm_w, lam_init, w_out):
    o_a = _rmsnorm(o_a, diff_norm_w) * (1.0 - lam_init)
    return jnp.concatenate([_merge(o_a), _merge(o_b), _merge(o_c)], axis=-1) @ w_out


def _conv_ffn(h, w_up, conv_w, conv_b, w_down):
    u = h @ w_up
    u = lax.conv_general_dilated(u, conv_w[:, None, :], window_strides=(1,),
                                 padding=((CONV_W // 2, CONV_W // 2),),
                                 dimension_numbers=('NWC', 'WIO', 'NWC'),
                                 feature_group_count=2 * D_FF) + conv_b
    g, val = jnp.split(u, 2, axis=-1)
    return (jax.nn.silu(g) * val) @ w_down


def _ada(cond, w_ada, b_ada):
    return jnp.split(jax.nn.silu(cond) @ w_ada + b_ada, 6, axis=-1)


def _modulate(x, shift, scale):
    return _layernorm(x) * (1.0 + scale) + shift


def _post_norm(x, y, g, b):
    return _layernorm(DEEPNORM_ALPHA * x + y) * g + b


def setup_inputs(seed: int = 0) -> dict:
    key = jax.random.key(seed)
    ks = jax.random.split(key, 26)
    f32 = jnp.float32

    def nrm(k, shape, s):
        return jax.random.normal(k, shape, f32) * s

    def gain(k, shape):
        return 1.0 + nrm(k, shape, 0.02)

    L = DEPTH
    return {
        "x": nrm(ks[0], (BATCH, SEQ, D_MODEL), 1.0),
        "c": nrm(ks[1], (BATCH, D_MODEL), 1.0),
        "ctx": nrm(ks[2], (BATCH, CTX_LEN, D_MODEL), 1.0),
        "c_ctx": nrm(ks[3], (D_MODEL,), 1.0),
        "w_ada": nrm(ks[4], (L, D_MODEL, 6 * D_MODEL), D_MODEL ** -0.5),
        "b_ada": nrm(ks[5], (L, 6 * D_MODEL), 0.02),
        "w_in": nrm(ks[6], (L, D_MODEL, W_IN_COLS), D_MODEL ** -0.5),
        "lam_q1": nrm(ks[7], (L, DA_QK), 0.1),
        "lam_k1": nrm(ks[8], (L, DA_QK), 0.1),
        "lam_q2": nrm(ks[9], (L, DA_QK), 0.1),
        "lam_k2": nrm(ks[10], (L, DA_QK), 0.1),
        "diff_norm_w": gain(ks[11], (L, DA_V)),
        "na_rpb": nrm(ks[12], (L, NA_HEADS, 2 * NA_KH - 1, 2 * NA_KW - 1), 0.1),
        "mla_q_norm_w": gain(ks[13], (L, MLA_Q_RANK)),
        "mla_kv_norm_w": gain(ks[14], (L, MLA_KV_RANK)),
        "w_uq": nrm(ks[15], (L, MLA_Q_RANK, MLA_HEADS * (MLA_NOPE + MLA_ROPE)), MLA_Q_RANK ** -0.5),
        "w_ukv": nrm(ks[16], (L, MLA_KV_RANK, MLA_HEADS * (MLA_NOPE + MLA_V)), MLA_KV_RANK ** -0.5),
        "w_out": nrm(ks[17], (L, MIX_WIDTH, D_MODEL), MIX_WIDTH ** -0.5 * DEEPNORM_BETA),
        "ln1_g": gain(ks[18], (L, D_MODEL)),
        "ln1_b": nrm(ks[19], (L, D_MODEL), 0.02),
        "w_up": nrm(ks[20], (L, D_MODEL, 2 * D_FF), D_MODEL ** -0.5),
        "conv_w": nrm(ks[21], (L, CONV_W, 2 * D_FF), CONV_W ** -0.5),
        "conv_b": nrm(ks[22], (L, 2 * D_FF), 0.02),
        "w_down": nrm(ks[23], (L, D_FF, D_MODEL), D_FF ** -0.5 * DEEPNORM_BETA),
        "ln2_g": gain(ks[24], (L, D_MODEL)),
        "ln2_b": nrm(ks[25], (L, D_MODEL), 0.02),
    }


def reference(x, c, ctx, c_ctx, w_ada, b_ada, w_in, lam_q1, lam_k1, lam_q2, lam_k2,
              diff_norm_w, na_rpb, mla_q_norm_w, mla_kv_norm_w, w_uq, w_ukv, w_out,
              ln1_g, ln1_b, w_up, conv_w, conv_b, w_down, ln2_g, ln2_b):
    n = x.shape[1]
    tabs = _axial_tables(n, x.dtype)
    for l in range(DEPTH):
        sa, ca, ga, sf, cf, gf = _ada(c[:, None, :], w_ada[l], b_ada[l])
        sa_c, ca_c, ga_c, sf_c, cf_c, gf_c = _ada(c_ctx, w_ada[l], b_ada[l])
        h = _modulate(x, sa, ca)
        h_c = _modulate(ctx, sa_c, ca_c)
        aq, ak, av, nq, nk, nv, mq, mk, mv = _split_heads(
            h @ w_in[l], tabs, mla_q_norm_w[l], mla_kv_norm_w[l], w_uq[l], w_ukv[l])
        caq, cak, cav, cnq, cnk, cnv, cmq, cmk, cmv = _split_heads(
            h_c @ w_in[l], None, mla_q_norm_w[l], mla_kv_norm_w[l], w_uq[l], w_ukv[l])
        lam_init = 0.8 - 0.6 * math.exp(-0.3 * l)
        lam = (jnp.exp(jnp.sum(lam_q1[l] * lam_k1[l])) - jnp.exp(jnp.sum(lam_q2[l] * lam_k2[l]))
               + lam_init)

        ak_all = jnp.concatenate([cak, ak], axis=2)
        av_all = jnp.concatenate([cav, av], axis=2)
        mk_all = jnp.concatenate([cmk, mk], axis=2)
        mv_all = jnp.concatenate([cmv, mv], axis=2)
        o_a = _sweep_query_blocks(lambda qb: _diff_attend(qb, ak_all, av_all, lam, DA_SCALE), aq)
        o_b = _neighbourhood_attend(nq, nk, nv, cnk, cnv, na_rpb[l], NA_SCALE)
        o_c = _sweep_query_blocks(lambda qb: _softmax_attend(qb, mk_all, mv_all, MLA_SCALE), mq)
        mix = _mix_out(o_a, o_b, o_c, diff_norm_w[l], lam_init, w_out[l])
        x = _post_norm(x, ga * mix, ln1_g[l], ln1_b[l])
        y = _conv_ffn(_modulate(x, sf, cf), w_up[l], conv_w[l], conv_b[l], w_down[l])
        x = _post_norm(x, gf * y, ln2_g[l], ln2_b[l])

        if l < DEPTH - 1:
            co_a = _diff_attend(caq, cak, cav, lam, DA_SCALE)
            co_b = _softmax_attend(cnq, cnk, cnv, NA_SCALE)
            co_c = _softmax_attend(cmq, cmk, cmv, MLA_SCALE)
            cmix = _mix_out(co_a, co_b, co_c, diff_norm_w[l], lam_init, w_out[l])
            ctx = _post_norm(ctx, ga_c * cmix, ln1_g[l], ln1_b[l])
            cy = _conv_ffn(_modulate(ctx, sf_c, cf_c), w_up[l], conv_w[l], conv_b[l], w_down[l])
            ctx = _post_norm(ctx, gf_c * cy, ln2_g[l], ln2_b[l])
    return x
```

```python
import functools
import math

import numpy as np
import jax
import jax.numpy as jnp
from jax import lax
from jax.experimental import pallas as pl
from jax.experimental.pallas import tpu as pltpu

F32 = jnp.float32
BF16 = jnp.bfloat16

GRID_W = 64
ROPE_DIM = 32
ROPE_BASE = 10000.0
DA_HEADS, DA_QK, DA_V = 6, 32, 64
NA_HEADS, NA_DIM, NA_KH, NA_KW = 6, 64, 8, 16
MLA_HEADS, MLA_Q_RANK, MLA_KV_RANK, MLA_NOPE, MLA_ROPE, MLA_V = 4, 256, 128, 64, 32, 64
D_FF = 2816
LN_EPS = 1e-6
NEG_INF = -1e30
DA_SCALE = DA_QK ** -0.5
NA_SCALE = NA_DIM ** -0.5
MLA_SCALE = (MLA_NOPE + MLA_ROPE) ** -0.5

T = 256
LANES = 128
VT_ROWS = 80
FF_CHUNK = 256
NA_WIN_ROWS = 12
HALO = 8
VMEM_LIMIT = 56 * 1024 * 1024

_NT = (((1,), (1,)), ((), ()))


def _layernorm(x):
    mu = jnp.mean(x, axis=-1, keepdims=True)
    xc = x - mu
    var = jnp.mean(xc * xc, axis=-1, keepdims=True)
    return xc * lax.rsqrt(var + LN_EPS)


def _rms(x, g):
    return (x * lax.rsqrt(jnp.mean(x * x, axis=-1, keepdims=True) + LN_EPS)) * g


def _cparams(sem):
    return pltpu.CompilerParams(dimension_semantics=sem, vmem_limit_bytes=VMEM_LIMIT)


def _ada_kernel(cond_ref, w_ref, b_ref, o_ref):
    c = cond_ref[...]
    a = (c * jax.nn.sigmoid(c)).astype(BF16)
    o_ref[0] = jnp.dot(a, w_ref[0].astype(BF16), preferred_element_type=F32) + b_ref[0]


def _ada(cond8, w_ada, b_ada):
    L, D, N = w_ada.shape
    tn = 1536
    return pl.pallas_call(
        _ada_kernel,
        grid=(L, N // tn),
        in_specs=[pl.BlockSpec((8, D), lambda l, j: (0, 0)),
                  pl.BlockSpec((1, D, tn), lambda l, j: (l, 0, j)),
                  pl.BlockSpec((1, 1, tn), lambda l, j: (l, 0, j))],
        out_specs=pl.BlockSpec((1, 8, tn), lambda l, j: (l, 0, j)),
        out_shape=jax.ShapeDtypeStruct((L, 8, N), F32),
        compiler_params=_cparams(("arbitrary", "arbitrary")),
        name="ada",
    )(cond8, w_ada, b_ada.reshape(L, 1, N))


def _inproj_kernel(x_ref, mod_ref, cosa_ref, sina_ref, cosm_ref, sinm_ref, w_ref, wvt_ref,
                   gq_ref, gkv_ref, wuq_ref, wukvk_ref, wukvvt_ref,
                   qa_ref, ka_ref, qn_ref, kn_ref, vn_ref, qm_ref, km_ref, vt_ref):
    D = x_ref.shape[-1]
    x = x_ref[0]
    mod = mod_ref[0, 0]
    h = (_layernorm(x) * (1.0 + mod[:, D:2 * D]) + mod[:, 0:D]).astype(BF16)
    proj = jnp.dot(h, w_ref[...], preferred_element_type=F32)

    cosa, sina = cosa_ref[...], sina_ref[...]
    cos3 = jnp.concatenate([cosa] * 3, axis=1)
    sin3 = jnp.concatenate([sina] * 3, axis=1)
    qa_ref[0] = ((proj[:, 0:384] * cos3 + proj[:, 384:768] * sin3) * DA_SCALE).astype(BF16)
    ka_ref[0] = (proj[:, 768:1152] * cos3 + proj[:, 1152:1536] * sin3).astype(BF16)
    qn_ref[0] = (proj[:, 1536:1920] * NA_SCALE).astype(BF16)
    kn_ref[0] = proj[:, 1920:2304].astype(BF16)
    vn_ref[0] = proj[:, 2304:2688].astype(BF16)

    cosm, sinm = cosm_ref[...], sinm_ref[...]
    cos4 = jnp.concatenate([cosm] * 4, axis=1)
    sin4 = jnp.concatenate([sinm] * 4, axis=1)
    cqn = _rms(proj[:, 2688:2944], gq_ref[...]).astype(BF16)
    qc = jnp.dot(cqn, wuq_ref[...], preferred_element_type=F32)
    qm_ref[0] = ((qc[:, 0:512] * cos4 + qc[:, 512:1024] * sin4) * MLA_SCALE).astype(BF16)
    kvn = _rms(proj[:, 2944:3072], gkv_ref[...]).astype(BF16)
    kr = proj[:, 3072:3200] * cosm + proj[:, 3200:3328] * sinm
    km = jnp.dot(kvn, wukvk_ref[...], preferred_element_type=F32) + jnp.concatenate([kr] * 4, axis=1)
    km_ref[0] = km.astype(BF16)

    vta = lax.dot_general(wvt_ref[...], h, _NT, preferred_element_type=F32)
    vtm = lax.dot_general(wukvvt_ref[...], kvn, _NT, preferred_element_type=F32)
    ones = jnp.ones((VT_ROWS - DA_V, x.shape[0]), BF16)
    for hd in range(DA_HEADS):
        vt_ref[0, hd, 0, 0:DA_V, :] = vta[hd * DA_V:(hd + 1) * DA_V].astype(BF16)
        vt_ref[0, hd, 0, DA_V:VT_ROWS, :] = ones
    for hd in range(MLA_HEADS):
        vt_ref[0, DA_HEADS + hd, 0, 0:MLA_V, :] = vtm[hd * MLA_V:(hd + 1) * MLA_V].astype(BF16)
        vt_ref[0, DA_HEADS + hd, 0, MLA_V:VT_ROWS, :] = ones


def _inproj(xs, mod_tok, tabs, wts):
    B, S, D = xs.shape
    nt = S // T
    cosa, sina, cosm, sinm = tabs
    w_ext, wvt, gq, gkv, wuq, wukvk, wukvvt = wts
    tok = lambda w: pl.BlockSpec((1, T, w), lambda b, i: (b, i, 0))
    tab = pl.BlockSpec((T, LANES), lambda b, i: (i, 0))
    full = lambda a: pl.BlockSpec(a.shape, lambda b, i: (0,) * a.ndim)
    nvt = DA_HEADS + MLA_HEADS
    out_shape = [jax.ShapeDtypeStruct((B, S, w), BF16) for w in (384, 384, 384, 384, 384, 512, 512)]
    out_shape.append(jax.ShapeDtypeStruct((B, nvt, nt, VT_ROWS, T), BF16))
    out_specs = [tok(w) for w in (384, 384, 384, 384, 384, 512, 512)]
    out_specs.append(pl.BlockSpec((1, nvt, 1, VT_ROWS, T), lambda b, i: (b, 0, i, 0, 0)))
    return pl.pallas_call(
        _inproj_kernel,
        grid=(B, nt),
        in_specs=[tok(D),
                  pl.BlockSpec((1, 1, 1, 6 * D), lambda b, i: (b, jnp.minimum(i, 1), 0, 0)),
                  tab, tab, tab, tab,
                  full(w_ext), full(wvt), full(gq), full(gkv), full(wuq), full(wukvk), full(wukvvt)],
        out_specs=out_specs,
        out_shape=out_shape,
        compiler_params=_cparams(("arbitrary", "arbitrary")),
        name="inproj",
    )(xs, mod_tok, cosa, sina, cosm, sinm, w_ext, wvt, gq, gkv, wuq, wukvk, wukvvt)


def _flash_t(qs, k_tile_fn, vt_tile_fn, nkv, m_sc, acc_sc):
    nsub = len(qs)
    for s in range(nsub):
        m_sc[s] = jnp.full(m_sc.shape[1:], NEG_INF, F32)
        acc_sc[s] = jnp.zeros(acc_sc.shape[1:], F32)

    def body(j, carry):
        for s in range(nsub):
            st = lax.dot_general(k_tile_fn(j, s), qs[s], _NT, preferred_element_type=F32)
            m_old = m_sc[s]
            m_new = jnp.maximum(m_old, jnp.max(st, axis=0, keepdims=True))
            alpha = jnp.exp(m_old - m_new)
            p = jnp.exp(st - m_new).astype(BF16)
            pv = jnp.dot(vt_tile_fn(j, s), p, preferred_element_type=F32)
            acc_sc[s] = alpha * acc_sc[s] + pv
            m_sc[s] = m_new
        return carry

    lax.fori_loop(0, nkv, body, 0)
    outs = []
    for s in range(nsub):
        a = acc_sc[s]
        outs.append(a[0:64] / a[64:65])
    return outs


def _diff_kernel(q_ref, k_ref, vt_ref, qmask_ref, lamp_ref, dnw_ref, o_ref, m_sc, acc_sc):
    i = pl.program_id(2)
    nkv = jnp.where(i == 0, 1, k_ref.shape[1] // T)
    qf = q_ref[0].astype(F32)
    qs = [(qf * qmask_ref[s:s + 1, :]).astype(BF16) for s in range(4)]

    def k_tile(j, s):
        return k_ref[0, pl.ds(pl.multiple_of(j * T, T), T), :]

    def vt_tile(j, s):
        return vt_ref[0, s // 2, j]

    outs = _flash_t(qs, k_tile, vt_tile, nkv, m_sc, acc_sc)
    lp = lamp_ref[0]
    lam_init = lp[4:5, 0:1]
    lam = (jnp.exp(jnp.sum(lp[0:1] * lp[1:2], axis=1, keepdims=True))
           - jnp.exp(jnp.sum(lp[2:3] * lp[3:4], axis=1, keepdims=True)) + lam_init)
    ys = []
    for hd in range(2):
        o = outs[2 * hd] - lam * outs[2 * hd + 1]
        ys.append(o * lax.rsqrt(jnp.mean(o * o, axis=0, keepdims=True) + LN_EPS))
    y = jnp.concatenate(ys, axis=0).T
    o_ref[0] = ((y * dnw_ref[0]) * (1.0 - lam_init)).astype(o_ref.dtype)


def _diff_attn(qa, ka, vt, qmask, lamp_l, dnw_l):
    B, S, _ = qa.shape
    nt = S // T
    G = DA_HEADS // 2
    return pl.pallas_call(
        _diff_kernel,
        grid=(B, G, nt),
        in_specs=[pl.BlockSpec((1, T, LANES), lambda b, g, i: (b, i, g)),
                  pl.BlockSpec((1, S, LANES), lambda b, g, i: (b, 0, g)),
                  pl.BlockSpec((1, 2, nt, VT_ROWS, T), lambda b, g, i: (b, g, 0, 0, 0)),
                  pl.BlockSpec((8, LANES), lambda b, g, i: (0, 0)),
                  pl.BlockSpec((1, 8, DA_QK), lambda b, g, i: (0, 0, 0)),
                  pl.BlockSpec((1, 1, LANES), lambda b, g, i: (0, 0, 0))],
        out_specs=pl.BlockSpec((1, T, LANES), lambda b, g, i: (b, i, g)),
        out_shape=jax.ShapeDtypeStruct((B, S, DA_HEADS * DA_V), BF16),
        scratch_shapes=[pltpu.VMEM((4, 1, T), F32), pltpu.VMEM((4, VT_ROWS, T), F32)],
        compiler_params=_cparams(("arbitrary", "arbitrary", "arbitrary")),
        name="diff_attn",
    )(qa, ka, vt, qmask, lamp_l, dnw_l)


def _mla_kernel(q_ref, k_ref, vt_ref, o_ref, m_sc, acc_sc):
    i = pl.program_id(2)
    nkv = jnp.where(i == 0, 1, k_ref.shape[1] // T)
    q = q_ref[0]
    qs = [q[:, 0:LANES], q[:, LANES:2 * LANES]]

    def k_tile(j, s):
        return k_ref[0, pl.ds(pl.multiple_of(j * T, T), T), s * LANES:(s + 1) * LANES]

    def vt_tile(j, s):
        return vt_ref[0, s, j]

    outs = _flash_t(qs, k_tile, vt_tile, nkv, m_sc, acc_sc)
    o_ref[0] = jnp.concatenate(outs, axis=0).T.astype(o_ref.dtype)


def _mla_attn(qm, km, vt):
    B, S, _ = qm.shape
    nt = S // T
    G = MLA_HEADS // 2
    voff = DA_HEADS // 2
    return pl.pallas_call(
        _mla_kernel,
        grid=(B, G, nt),
        in_specs=[pl.BlockSpec((1, T, 2 * LANES), lambda b, g, i: (b, i, g)),
                  pl.BlockSpec((1, S, 2 * LANES), lambda b, g, i: (b, 0, g)),
                  pl.BlockSpec((1, 2, nt, VT_ROWS, T), lambda b, g, i: (b, voff + g, 0, 0, 0))],
        out_specs=pl.BlockSpec((1, T, LANES), lambda b, g, i: (b, i, g)),
        out_shape=jax.ShapeDtypeStruct((B, S, MLA_HEADS * MLA_V), BF16),
        scratch_shapes=[pltpu.VMEM((2, 1, T), F32), pltpu.VMEM((2, VT_ROWS, T), F32)],
        compiler_params=_cparams(("arbitrary", "arbitrary", "arbitrary")),
        name="mla_attn",
    )(qm, km, vt)


def _na_kernel(q_ref, k_ref, v_ref, bias_ref, hmask_ref, o_ref, *, ctx_len, rows):
    i = pl.program_id(2)
    qf = q_ref[0].astype(F32)
    lane_head = hmask_ref[...]
    k_ctx = k_ref[0, 0:ctx_len, :]
    v_ctx = v_ref[0, 0:ctx_len, :]

    @pl.when(i == 0)
    def _():
        out = jnp.zeros((T, LANES), F32)
        for hd in range(2):
            hm = lane_head[hd:hd + 1, :]
            qh = (qf * hm).astype(BF16)
            s = lax.dot_general(qh, k_ctx, _NT, preferred_element_type=F32)
            m = jnp.max(s, axis=1, keepdims=True)
            p = jnp.exp(s - m)
            l = jnp.sum(p, axis=1, keepdims=True)
            o = jnp.dot(p.astype(BF16), v_ctx, preferred_element_type=F32) / l
            out = out + o * hm
        o_ref[0] = out.astype(o_ref.dtype)

    @pl.when(i > 0)
    def _():
        r = (i - 1) * (T // GRID_W)
        start = jnp.clip(r - NA_KH // 2, 0, rows - NA_WIN_ROWS)
        off = pl.multiple_of(ctx_len + start * GRID_W, GRID_W)
        k_win = k_ref[0, pl.ds(off, NA_WIN_ROWS * GRID_W), :]
        v_win = v_ref[0, pl.ds(off, NA_WIN_ROWS * GRID_W), :]
        out = jnp.zeros((T, LANES), F32)
        for hd in range(2):
            hm = lane_head[hd:hd + 1, :]
            qh = (qf * hm).astype(BF16)
            s_w = lax.dot_general(qh, k_win, _NT, preferred_element_type=F32) + bias_ref[0, hd, 0]
            s_c = lax.dot_general(qh, k_ctx, _NT, preferred_element_type=F32)
            m = jnp.maximum(jnp.max(s_w, axis=1, keepdims=True), jnp.max(s_c, axis=1, keepdims=True))
            p_w = jnp.exp(s_w - m)
            p_c = jnp.exp(s_c - m)
            l = jnp.sum(p_w, axis=1, keepdims=True) + jnp.sum(p_c, axis=1, keepdims=True)
            o = (jnp.dot(p_w.astype(BF16), v_win, preferred_element_type=F32)
                 + jnp.dot(p_c.astype(BF16), v_ctx, preferred_element_type=F32)) / l
            out = out + o * hm
        o_ref[0] = out.astype(o_ref.dtype)


def _na_attn(qn, kn, vn, bias_l, hmask, ctx_len):
    B, S, _ = qn.shape
    nt = S // T
    rows = (S - ctx_len) // GRID_W
    G = NA_HEADS // 2
    nlat = nt - ctx_len // T

    def bias_map(b, g, i):
        variant = jnp.where(i <= 1, 0, jnp.where(i == nlat, 2, 1))
        return (0, g, variant, 0, 0)

    return pl.pallas_call(
        functools.partial(_na_kernel, ctx_len=ctx_len, rows=rows),
        grid=(B, G, nt),
        in_specs=[pl.BlockSpec((1, T, LANES), lambda b, g, i: (b, i, g)),
                  pl.BlockSpec((1, S, LANES), lambda b, g, i: (b, 0, g)),
                  pl.BlockSpec((1, S, LANES), lambda b, g, i: (b, 0, g)),
                  pl.BlockSpec((1, 2, 1, T, NA_WIN_ROWS * GRID_W), bias_map),
                  pl.BlockSpec((8, LANES), lambda b, g, i: (0, 0))],
        out_specs=pl.BlockSpec((1, T, LANES), lambda b, g, i: (b, i, g)),
        out_shape=jax.ShapeDtypeStruct((B, S, NA_HEADS * NA_DIM), BF16),
        compiler_params=_cparams(("arbitrary", "arbitrary", "arbitrary")),
        name="na_attn",
    )(qn, kn, vn, bias_l, hmask)


def _outproj_kernel(x_ref, ma_ref, mb_ref, mc_ref, mod_ref, w_ref, g_ref, b_ref, o_ref, *, alpha):
    D = x_ref.shape[-1]
    mix = jnp.concatenate([ma_ref[0], mb_ref[0], mc_ref[0]], axis=1)
    y = jnp.dot(mix, w_ref[...], preferred_element_type=F32)
    gate = mod_ref[0, 0][:, 2 * D:3 * D]
    o_ref[0] = _layernorm(alpha * x_ref[0] + gate * y) * g_ref[...] + b_ref[...]


def _outproj(xs, mix_a, mix_b, mix_c, mod_tok, w_out, g1, b1, alpha):
    B, S, D = xs.shape
    nt = S // T
    tok = lambda w: pl.BlockSpec((1, T, w), lambda b, i: (b, i, 0))
    full = lambda a: pl.BlockSpec(a.shape, lambda b, i: (0,) * a.ndim)
    return pl.pallas_call(
        functools.partial(_outproj_kernel, alpha=alpha),
        grid=(B, nt),
        in_specs=[tok(D), tok(mix_a.shape[-1]), tok(mix_b.shape[-1]), tok(mix_c.shape[-1]),
                  pl.BlockSpec((1, 1, 1, 6 * D), lambda b, i: (b, jnp.minimum(i, 1), 0, 0)),
                  full(w_out), full(g1), full(b1)],
        out_specs=tok(D),
        out_shape=jax.ShapeDtypeStruct((B, S, D), F32),
        compiler_params=_cparams(("arbitrary", "arbitrary")),
        name="outproj",
    )(xs, mix_a, mix_b, mix_c, mod_tok, w_out, g1, b1)


def _ffn_kernel(x_ref, xp_ref, xn_ref, mod_ref, wg_ref, wv_ref, cg_ref, cv_ref, wd_ref, g_ref, b_ref,
                o_ref, acc_ref, *, alpha, nlat_first, nlast):
    D = x_ref.shape[-1]
    i = pl.program_id(1)
    mod = mod_ref[0, 0]
    shift, scale, gate = mod[:, 3 * D:4 * D], mod[:, 4 * D:5 * D], mod[:, 5 * D:6 * D]
    x = x_ref[0]
    has_prev = jnp.logical_and(i != 0, i != nlat_first)
    has_next = jnp.logical_and(i != nlat_first - 1, i != nlast)
    hp = (_layernorm(xp_ref[0]) * (1.0 + scale) + shift) * jnp.where(has_prev, 1.0, 0.0)
    hn = (_layernorm(xn_ref[0]) * (1.0 + scale) + shift) * jnp.where(has_next, 1.0, 0.0)
    hc = _layernorm(x) * (1.0 + scale) + shift
    h = jnp.concatenate([hp, hc, hn], axis=0).astype(BF16)
    n = x.shape[0]
    acc_ref[...] = jnp.zeros_like(acc_ref)

    def conv(u, c):
        return (u[HALO - 1:HALO - 1 + n] * c[0:1] + u[HALO:HALO + n] * c[1:2]
                + u[HALO + 1:HALO + 1 + n] * c[2:3] + c[3:4])

    def body(j, carry):
        ug = conv(jnp.dot(h, wg_ref[j], preferred_element_type=F32), cg_ref[j])
        uv = conv(jnp.dot(h, wv_ref[j], preferred_element_type=F32), cv_ref[j])
        a = ((ug * jax.nn.sigmoid(ug)) * uv).astype(BF16)
        acc_ref[...] += jnp.dot(a, wd_ref[j], preferred_element_type=F32)
        return carry

    lax.fori_loop(0, wg_ref.shape[0], body, 0)
    o_ref[0] = _layernorm(alpha * x + gate * acc_ref[...]) * g_ref[...] + b_ref[...]


def _ffn(xs, mod_tok, wg, wv, cg, cv, wd, g2, b2, alpha, ctx_len):
    B, S, D = xs.shape
    nt = S // T
    th = T // HALO
    nh = S // HALO
    tok = pl.BlockSpec((1, T, D), lambda b, i: (b, i, 0))
    full = lambda a: pl.BlockSpec(a.shape, lambda b, i: (0,) * a.ndim)
    return pl.pallas_call(
        functools.partial(_ffn_kernel, alpha=alpha, nlat_first=ctx_len // T, nlast=nt - 1),
        grid=(B, nt),
        in_specs=[tok,
                  pl.BlockSpec((1, HALO, D), lambda b, i: (b, jnp.maximum(i * th - 1, 0), 0)),
                  pl.BlockSpec((1, HALO, D), lambda b, i: (b, jnp.minimum((i + 1) * th, nh - 1), 0)),
                  pl.BlockSpec((1, 1, 1, 6 * D), lambda b, i: (b, jnp.minimum(i, 1), 0, 0)),
                  full(wg), full(wv), full(cg), full(cv), full(wd), full(g2), full(b2)],
        out_specs=tok,
        out_shape=jax.ShapeDtypeStruct((B, S, D), F32),
        scratch_shapes=[pltpu.VMEM((T, D), F32)],
        compiler_params=_cparams(("arbitrary", "arbitrary")),
        name="ffn",
    )(xs, xs, xs, mod_tok, wg, wv, cg, cv, wd, g2, b2)


def _rot_perm(width):
    j = np.arange(width)
    jj = j % (ROPE_DIM // 2)
    lo = jj < ROPE_DIM // 4
    src = np.where(lo, j + ROPE_DIM // 4, j - ROPE_DIM // 4)
    sign = np.where(lo, -1.0, 1.0).astype(np.float32)
    return src, sign


def _rope_tables(n, ctx_len):
    t = jnp.arange(n)
    row = (t // GRID_W).astype(F32)
    col = (t % GRID_W).astype(F32)
    axis_dim = ROPE_DIM // 2
    inv_freq = ROPE_BASE ** (-jnp.arange(0, axis_dim, 2, dtype=F32) / axis_dim)

    def cs(pos):
        ang = pos[:, None] * inv_freq[None, :]
        ang = jnp.concatenate([ang, ang], axis=-1)
        return jnp.cos(ang), jnp.sin(ang)

    cr, sr = cs(row)
    cc, sc = cs(col)
    cos32 = jnp.concatenate([cr, cc], axis=-1)
    sin32 = jnp.concatenate([sr, sc], axis=-1)
    cos32 = jnp.concatenate([jnp.ones((ctx_len, ROPE_DIM), F32), cos32], axis=0)
    sin32 = jnp.concatenate([jnp.zeros((ctx_len, ROPE_DIM), F32), sin32], axis=0)
    S = n + ctx_len
    cosa = jnp.tile(cos32, (1, LANES // ROPE_DIM))
    sina = jnp.tile(sin32, (1, LANES // ROPE_DIM))
    cosm = jnp.concatenate([jnp.ones((S, MLA_NOPE), F32), cos32, jnp.ones((S, LANES - MLA_NOPE - MLA_ROPE), F32)], axis=1)
    sinm = jnp.concatenate([jnp.zeros((S, MLA_NOPE), F32), sin32, jnp.zeros((S, LANES - MLA_NOPE - MLA_ROPE), F32)], axis=1)
    return cosa, sina, cosm, sinm


def _na_bias_index(rows):
    qrows = T // GRID_W
    nk = NA_WIN_ROWS * GRID_W
    dr = np.zeros((3, T, nk), np.int32)
    dc = np.zeros((3, T, nk), np.int32)
    ok = np.zeros((3, T, nk), bool)
    qc = np.arange(GRID_W)
    c0 = np.clip(qc - NA_KW // 2, 0, GRID_W - NA_KW)
    band = (qc[None, :] >= c0[:, None]) & (qc[None, :] < c0[:, None] + NA_KW)
    dcm = np.clip(qc[None, :] - qc[:, None], -(NA_KW - 1), NA_KW - 1) + (NA_KW - 1)
    for v, r in enumerate((0, 2 * qrows, rows - qrows)):
        start = int(np.clip(r - NA_KH // 2, 0, rows - NA_WIN_ROWS))
        for a in range(qrows):
            qr = r + a
            r0 = int(np.clip(qr - NA_KH // 2, 0, rows - NA_KH))
            for kb in range(NA_WIN_ROWS):
                kr = start + kb
                qs_, ks_ = slice(a * GRID_W, (a + 1) * GRID_W), slice(kb * GRID_W, (kb + 1) * GRID_W)
                if r0 <= kr < r0 + NA_KH:
                    dr[v, qs_, ks_] = kr - qr + (NA_KH - 1)
                    dc[v, qs_, ks_] = dcm
                    ok[v, qs_, ks_] = band
    return dr, dc, ok


def kernel(x, c, ctx, c_ctx, w_ada, b_ada, w_in, lam_q1, lam_k1, lam_q2, lam_k2, diff_norm_w, na_rpb,
           mla_q_norm_w, mla_kv_norm_w, w_uq, w_ukv, w_out, ln1_g, ln1_b, w_up, conv_w, conv_b, w_down,
           ln2_g, ln2_b):
    B, N, D = x.shape
    C = ctx.shape[1]
    L = w_in.shape[0]
    S = C + N
    rows = N // GRID_W
    assert C == T and N % T == 0 and rows >= NA_WIN_ROWS and B + 1 <= 8 and D_FF % FF_CHUNK == 0
    alpha = (2 * L) ** 0.25

    cond8 = jnp.zeros((8, D), F32).at[0:B].set(c).at[B].set(c_ctx)
    mod_all = _ada(cond8, w_ada, b_ada)
    ctx_rows = jnp.broadcast_to(mod_all[:, B][:, None], (L, B, 6 * D))
    mod_tok_all = jnp.stack([ctx_rows, mod_all[:, 0:B]], axis=2).reshape(L, B, 2, 1, 6 * D)

    tabs = _rope_tables(N, C)
    lane = np.arange(LANES)
    qmask = np.zeros((8, LANES), np.float32)
    for s in range(4):
        qmask[s] = (lane // DA_QK == s)
    hmask = np.zeros((8, LANES), np.float32)
    for hd in range(2):
        hmask[hd] = (lane // NA_DIM == hd)
    qmask, hmask = jnp.asarray(qmask), jnp.asarray(hmask)
    dr, dc, ok = _na_bias_index(rows)

    src_a, sgn_a = _rot_perm(DA_HEADS * 2 * DA_QK)
    src_r, sgn_r = _rot_perm(MLA_ROPE)
    sp = np.cumsum([0, 384, 384, 384, 384, 384, 384, MLA_Q_RANK, MLA_KV_RANK, MLA_ROPE])
    zeros = lambda *s: jnp.zeros(s, F32)

    xs = jnp.concatenate([ctx, x], axis=1)
    for l in range(L):
        wi = w_in[l]
        a_q, a_k, a_v, n_q, n_k, n_v, c_q, c_kv, k_r = [wi[:, sp[t]:sp[t + 1]] for t in range(9)]
        kr_pad = jnp.concatenate([zeros(D, MLA_NOPE), k_r, zeros(D, LANES - MLA_NOPE - MLA_ROPE)], axis=1)
        krr_pad = jnp.concatenate([zeros(D, MLA_NOPE), k_r[:, src_r] * sgn_r,
                                   zeros(D, LANES - MLA_NOPE - MLA_ROPE)], axis=1)
        w_ext = jnp.concatenate([a_q, a_q[:, src_a] * sgn_a, a_k, a_k[:, src_a] * sgn_a, n_q, n_k, n_v,
                                 c_q, c_kv, kr_pad, krr_pad], axis=1).astype(BF16)
        wvt = a_v.T.astype(BF16)
        uq = w_uq[l].reshape(MLA_Q_RANK, MLA_HEADS, MLA_NOPE + MLA_ROPE)
        uq_rope = uq[:, :, MLA_NOPE:]
        padq = jnp.zeros((MLA_Q_RANK, MLA_HEADS, LANES - MLA_NOPE - MLA_ROPE), F32)
        wuq = jnp.concatenate([
            jnp.concatenate([uq, padq], axis=2).reshape(MLA_Q_RANK, MLA_HEADS * LANES),
            jnp.concatenate([jnp.zeros_like(uq[:, :, :MLA_NOPE]), uq_rope[:, :, src_r] * sgn_r, padq],
                            axis=2).reshape(MLA_Q_RANK, MLA_HEADS * LANES)], axis=1).astype(BF16)
        ukv = w_ukv[l].reshape(MLA_KV_RANK, MLA_HEADS, MLA_NOPE + MLA_V)
        wukvk = jnp.concatenate([ukv[:, :, :MLA_NOPE], jnp.zeros((MLA_KV_RANK, MLA_HEADS, LANES - MLA_NOPE), F32)],
                                axis=2).reshape(MLA_KV_RANK, MLA_HEADS * LANES).astype(BF16)
        wukvvt = ukv[:, :, MLA_NOPE:].reshape(MLA_KV_RANK, MLA_HEADS * MLA_V).T.astype(BF16)
        gq = mla_q_norm_w[l].reshape(1, MLA_Q_RANK)
        gkv = mla_kv_norm_w[l].reshape(1, MLA_KV_RANK)
        mod_tok = mod_tok_all[l]

        qa, ka, qn, kn, vn, qm, km, vt = _inproj(xs, mod_tok, tabs, (w_ext, wvt, gq, gkv, wuq, wukvk, wukvvt))

        lam_init = 0.8 - 0.6 * math.exp(-0.3 * l)
        lamp = jnp.zeros((1, 8, DA_QK), F32)
        lamp = lamp.at[0, 0].set(lam_q1[l]).at[0, 1].set(lam_k1[l]).at[0, 2].set(lam_q2[l]).at[0, 3].set(lam_k2[l])
        lamp = lamp.at[0, 4].set(lam_init)
        dnw = jnp.concatenate([diff_norm_w[l], diff_norm_w[l]]).reshape(1, 1, LANES)
        mix_a = _diff_attn(qa, ka, vt, qmask, lamp, dnw)

        bias = jnp.where(ok, na_rpb[l][:, dr, dc], NEG_INF)[None]
        mix_b = _na_attn(qn, kn, vn, bias, hmask, C)
        mix_c = _mla_attn(qm, km, vt)

        x1 = _outproj(xs, mix_a, mix_b, mix_c, mod_tok, w_out[l].astype(BF16),
                      ln1_g[l].reshape(1, D), ln1_b[l].reshape(1, D), alpha)

        nck = D_FF // FF_CHUNK
        wu = w_up[l].astype(BF16)
        wg = wu[:, :D_FF].reshape(D, nck, FF_CHUNK).transpose(1, 0, 2)
        wv = wu[:, D_FF:].reshape(D, nck, FF_CHUNK).transpose(1, 0, 2)
        cw = jnp.concatenate([conv_w[l], conv_b[l][None], jnp.zeros((4, 2 * D_FF), F32)], axis=0)
        cg = cw[:, :D_FF].reshape(8, nck, FF_CHUNK).transpose(1, 0, 2)
        cv = cw[:, D_FF:].reshape(8, nck, FF_CHUNK).transpose(1, 0, 2)
        wd = w_down[l].astype(BF16).reshape(nck, FF_CHUNK, D)
        xs = _ffn(x1, mod_tok, wg, wv, cg, cv, wd, ln2_g[l].reshape(1, D), ln2_b[l].reshape(1, D), alpha, C)
    return xs[:, C:]
```

```python
import functools
import math

import numpy as np
import jax
import jax.numpy as jnp
from jax import lax
from jax.experimental import pallas as pl
from jax.experimental.pallas import tpu as pltpu

F32 = jnp.float32
BF16 = jnp.bfloat16

GRID_W = 64
ROPE_DIM = 32
ROPE_BASE = 10000.0
DA_HEADS, DA_QK, DA_V = 6, 32, 64
NA_HEADS, NA_DIM, NA_KH, NA_KW = 6, 64, 8, 16
MLA_HEADS, MLA_Q_RANK, MLA_KV_RANK, MLA_NOPE, MLA_ROPE, MLA_V = 4, 256, 128, 64, 32, 64
D_FF = 2816
LN_EPS = 1e-6
NEG_INF = -1e30
DA_SCALE = DA_QK ** -0.5
NA_SCALE = NA_DIM ** -0.5
MLA_SCALE = (MLA_NOPE + MLA_ROPE) ** -0.5
LOG2E = math.log2(math.e)

T = 256
KT = 2 * T
LANES = 128
VT_ROWS = 80
FF_CHUNK = 256
NA_WIN_ROWS = 12
HALO = 8
VMEM_LIMIT = 56 * 1024 * 1024

_NT = (((1,), (1,)), ((), ()))


def _layernorm(x):
    mu = jnp.mean(x, axis=-1, keepdims=True)
    xc = x - mu
    var = jnp.mean(xc * xc, axis=-1, keepdims=True)
    return xc * lax.rsqrt(var + LN_EPS)


def _rms(x, g):
    return (x * lax.rsqrt(jnp.mean(x * x, axis=-1, keepdims=True) + LN_EPS)) * g


def _cparams(sem):
    return pltpu.CompilerParams(dimension_semantics=sem, vmem_limit_bytes=VMEM_LIMIT)


def _ada_kernel(cond_ref, w_ref, b_ref, o_ref):
    c = cond_ref[...]
    a = (c * jax.nn.sigmoid(c)).astype(BF16)
    o_ref[0] = jnp.dot(a, w_ref[0].astype(BF16), preferred_element_type=F32) + b_ref[0]


def _ada(cond8, w_ada, b_ada):
    L, D, N = w_ada.shape
    tn = 1536
    return pl.pallas_call(
        _ada_kernel,
        grid=(L, N // tn),
        in_specs=[pl.BlockSpec((8, D), lambda l, j: (0, 0)),
                  pl.BlockSpec((1, D, tn), lambda l, j: (l, 0, j)),
                  pl.BlockSpec((1, 1, tn), lambda l, j: (l, 0, j))],
        out_specs=pl.BlockSpec((1, 8, tn), lambda l, j: (l, 0, j)),
        out_shape=jax.ShapeDtypeStruct((L, 8, N), F32),
        compiler_params=_cparams(("arbitrary", "arbitrary")),
        name="ada",
    )(cond8, w_ada, b_ada.reshape(L, 1, N))


def _inproj_kernel(x_ref, mod_ref, cosa_ref, sina_ref, cosm_ref, sinm_ref, w_ref, wvt_ref,
                   gq_ref, gkv_ref, wuq_ref, wukvk_ref, wukvvt_ref,
                   qa_ref, ka_ref, qn_ref, kn_ref, vn_ref, qm_ref, km_ref, vt_ref):
    D = x_ref.shape[-1]
    x = x_ref[0]
    mod = mod_ref[0, 0]
    h = (_layernorm(x) * (1.0 + mod[:, D:2 * D]) + mod[:, 0:D]).astype(BF16)
    proj = jnp.dot(h, w_ref[...], preferred_element_type=F32)

    cosa, sina = cosa_ref[...], sina_ref[...]
    cos3 = jnp.concatenate([cosa] * 3, axis=1)
    sin3 = jnp.concatenate([sina] * 3, axis=1)
    qa_ref[0] = ((proj[:, 0:384] * cos3 + proj[:, 384:768] * sin3) * (DA_SCALE * LOG2E)).astype(BF16)
    ka_ref[0] = (proj[:, 768:1152] * cos3 + proj[:, 1152:1536] * sin3).astype(BF16)
    qn_ref[0] = (proj[:, 1536:1920] * NA_SCALE).astype(BF16)
    kn_ref[0] = proj[:, 1920:2304].astype(BF16)
    vn_ref[0] = proj[:, 2304:2688].astype(BF16)

    cosm, sinm = cosm_ref[...], sinm_ref[...]
    cos4 = jnp.concatenate([cosm] * 4, axis=1)
    sin4 = jnp.concatenate([sinm] * 4, axis=1)
    cqn = _rms(proj[:, 2688:2944], gq_ref[...]).astype(BF16)
    qc = jnp.dot(cqn, wuq_ref[...], preferred_element_type=F32)
    qm_ref[0] = ((qc[:, 0:512] * cos4 + qc[:, 512:1024] * sin4) * (MLA_SCALE * LOG2E)).astype(BF16)
    kvn = _rms(proj[:, 2944:3072], gkv_ref[...]).astype(BF16)
    kr = proj[:, 3072:3200] * cosm + proj[:, 3200:3328] * sinm
    km = jnp.dot(kvn, wukvk_ref[...], preferred_element_type=F32) + jnp.concatenate([kr] * 4, axis=1)
    km_ref[0] = km.astype(BF16)

    vta = lax.dot_general(wvt_ref[...], h, _NT, preferred_element_type=F32)
    vtm = lax.dot_general(wukvvt_ref[...], kvn, _NT, preferred_element_type=F32)
    ones = jnp.ones((VT_ROWS - DA_V, x.shape[0]), BF16)
    for hd in range(MLA_HEADS):
        vt_ref[0, hd, 0, 0:MLA_V, :] = vtm[hd * MLA_V:(hd + 1) * MLA_V].astype(BF16)
        vt_ref[0, hd, 0, MLA_V:VT_ROWS, :] = ones
    for hd in range(DA_HEADS):
        vt_ref[0, MLA_HEADS + hd, 0, 0:DA_V, :] = vta[hd * DA_V:(hd + 1) * DA_V].astype(BF16)
        vt_ref[0, MLA_HEADS + hd, 0, DA_V:VT_ROWS, :] = ones


def _inproj(xs, mod_tok, tabs, wts):
    B, S, D = xs.shape
    nt = S // T
    cosa, sina, cosm, sinm = tabs
    w_ext, wvt, gq, gkv, wuq, wukvk, wukvvt = wts
    tok = lambda w: pl.BlockSpec((1, T, w), lambda b, i: (b, i, 0))
    tab = pl.BlockSpec((T, LANES), lambda b, i: (i, 0))
    full = lambda a: pl.BlockSpec(a.shape, lambda b, i: (0,) * a.ndim)
    nvt = DA_HEADS + MLA_HEADS
    out_shape = [jax.ShapeDtypeStruct((B, S, w), BF16) for w in (384, 384, 384, 384, 384, 512, 512)]
    out_shape.append(jax.ShapeDtypeStruct((B, nvt, nt, VT_ROWS, T), BF16))
    out_specs = [tok(w) for w in (384, 384, 384, 384, 384, 512, 512)]
    out_specs.append(pl.BlockSpec((1, nvt, 1, VT_ROWS, T), lambda b, i: (b, 0, i, 0, 0)))
    return pl.pallas_call(
        _inproj_kernel,
        grid=(B, nt),
        in_specs=[tok(D),
                  pl.BlockSpec((1, 1, 1, 6 * D), lambda b, i: (b, jnp.minimum(i, 1), 0, 0)),
                  tab, tab, tab, tab,
                  full(w_ext), full(wvt), full(gq), full(gkv), full(wuq), full(wukvk), full(wukvvt)],
        out_specs=out_specs,
        out_shape=out_shape,
        compiler_params=_cparams(("arbitrary", "arbitrary")),
        name="inproj",
    )(xs, mod_tok, cosa, sina, cosm, sinm, w_ext, wvt, gq, gkv, wuq, wukvk, wukvvt)


def _flash_t(qts, k_rows_fn, vt_tile_fn, ctx_only, nbig, scratch):
    s_sc, cm_sc, p_sc, al_sc = (scratch[0:2], scratch[2:4], scratch[4:6], scratch[6:8])
    m_sc, acc_sc = scratch[8:10]
    nsub = len(qts)

    def cols(s):
        return slice(s * T, (s + 1) * T)

    def scores(u, slot):
        start = pl.multiple_of(T + u * KT, T)
        for s in range(nsub):
            st = jnp.dot(k_rows_fn(start, KT, s), qts[s], preferred_element_type=F32)
            s_sc[slot][:, cols(s)] = st
            cm_sc[slot][:, cols(s)] = jnp.max(st, axis=0, keepdims=True)

    def softmax(slot):
        m_old = m_sc[...]
        m_new = jnp.maximum(m_old, cm_sc[slot][...])
        al_sc[slot][...] = jnp.exp2(m_old - m_new)
        p_sc[slot][...] = jnp.exp2(s_sc[slot][...] - m_new).astype(BF16)
        m_sc[...] = m_new

    def accumulate(u, slot):
        ja, jb = jnp.maximum(2 * u + 1, 0), 2 * u + 2
        for s in range(nsub):
            pv = (jnp.dot(vt_tile_fn(ja, s), p_sc[slot][0:T, cols(s)], preferred_element_type=F32)
                  + jnp.dot(vt_tile_fn(jb, s), p_sc[slot][T:KT, cols(s)], preferred_element_type=F32))
            acc_sc[:, cols(s)] = al_sc[slot][:, cols(s)] * acc_sc[:, cols(s)] + pv

    acc_sc[...] = jnp.zeros(acc_sc.shape, F32)
    al_sc[1][...] = jnp.ones(al_sc[1].shape, F32)
    p_sc[1][0:T, :] = jnp.zeros((T, nsub * T), BF16)
    for s in range(nsub):
        st = jnp.dot(k_rows_fn(0, T, s), qts[s], preferred_element_type=F32)
        m0 = jnp.max(st, axis=0, keepdims=True)
        m_sc[:, cols(s)] = m0
        p_sc[1][T:KT, cols(s)] = jnp.exp2(st - m0).astype(BF16)

    @pl.when(ctx_only)
    def _():
        accumulate(-1, 1)

    @pl.when(jnp.logical_not(ctx_only))
    def _():
        scores(0, 0)

        def body(q, carry):
            scores(2 * q + 1, 1)
            softmax(0)
            accumulate(2 * q - 1, 1)
            scores(2 * q + 2, 0)
            softmax(1)
            accumulate(2 * q, 0)
            return carry

        lax.fori_loop(0, nbig // 2 - 1, body, 0)
        scores(nbig - 1, 1)
        softmax(0)
        accumulate(nbig - 3, 1)
        softmax(1)
        accumulate(nbig - 2, 0)
        accumulate(nbig - 1, 1)

    acc = acc_sc[...]
    return [acc[0:64, cols(s)] / acc[64:65, cols(s)] for s in range(nsub)]


def _flash_scratch(nsub):
    w = nsub * T
    big, row = (KT, w), (1, w)
    return [pltpu.VMEM(big, F32), pltpu.VMEM(big, F32),
            pltpu.VMEM(row, F32), pltpu.VMEM(row, F32),
            pltpu.VMEM(big, BF16), pltpu.VMEM(big, BF16),
            pltpu.VMEM(row, F32), pltpu.VMEM(row, F32),
            pltpu.VMEM(row, F32), pltpu.VMEM((VT_ROWS, w), F32)]


def _diff_kernel(q_ref, k_ref, vt_ref, qmask_ref, lamp_ref, dnw_ref, o_ref, *scratch):
    ctx_only = pl.program_id(2) == 0
    nbig = (k_ref.shape[1] - T) // KT
    qf = q_ref[0].astype(F32)
    qts = [(qf * qmask_ref[s:s + 1, :]).T.astype(BF16) for s in range(4)]

    def k_rows(start, n, s):
        return k_ref[0, pl.ds(start, n), :]

    def vt_tile(j, s):
        return vt_ref[0, s // 2, j]

    outs = _flash_t(qts, k_rows, vt_tile, ctx_only, nbig, scratch)
    lp = lamp_ref[0]
    lam_init = lp[4:5, 0:1]
    lam = (jnp.exp(jnp.sum(lp[0:1] * lp[1:2], axis=1, keepdims=True))
           - jnp.exp(jnp.sum(lp[2:3] * lp[3:4], axis=1, keepdims=True)) + lam_init)
    ys = []
    for hd in range(2):
        o = outs[2 * hd] - lam * outs[2 * hd + 1]
        ys.append(o * lax.rsqrt(jnp.mean(o * o, axis=0, keepdims=True) + LN_EPS))
    y = jnp.concatenate(ys, axis=0).T
    o_ref[0] = ((y * dnw_ref[0]) * (1.0 - lam_init)).astype(o_ref.dtype)


def _diff_attn(qa, ka, vt, qmask, lamp_l, dnw_l):
    B, S, _ = qa.shape
    nt = S // T
    G = DA_HEADS // 2
    return pl.pallas_call(
        _diff_kernel,
        grid=(B, G, nt),
        in_specs=[pl.BlockSpec((1, T, LANES), lambda b, g, i: (b, i, g)),
                  pl.BlockSpec((1, S, LANES), lambda b, g, i: (b, 0, g)),
                  pl.BlockSpec((1, 2, nt, VT_ROWS, T), lambda b, g, i: (b, MLA_HEADS // 2 + g, 0, 0, 0)),
                  pl.BlockSpec((8, LANES), lambda b, g, i: (0, 0)),
                  pl.BlockSpec((1, 8, DA_QK), lambda b, g, i: (0, 0, 0)),
                  pl.BlockSpec((1, 1, LANES), lambda b, g, i: (0, 0, 0))],
        out_specs=pl.BlockSpec((1, T, LANES), lambda b, g, i: (b, i, g)),
        out_shape=jax.ShapeDtypeStruct((B, S, DA_HEADS * DA_V), BF16),
        scratch_shapes=_flash_scratch(4),
        compiler_params=_cparams(("arbitrary", "arbitrary", "arbitrary")),
        name="diff_attn",
    )(qa, ka, vt, qmask, lamp_l, dnw_l)


def _mla_kernel(q_ref, k_ref, vt_ref, o_ref, *scratch):
    ctx_only = pl.program_id(1) == 0
    nbig = (k_ref.shape[1] - T) // KT
    qf = q_ref[0].astype(F32)
    qts = [qf[:, s * LANES:(s + 1) * LANES].T.astype(BF16) for s in range(MLA_HEADS)]

    def k_rows(start, n, s):
        return k_ref[0, pl.ds(start, n), s * LANES:(s + 1) * LANES]

    def vt_tile(j, s):
        return vt_ref[0, s, j]

    outs = _flash_t(qts, k_rows, vt_tile, ctx_only, nbig, scratch)
    o_ref[0] = jnp.concatenate(outs, axis=0).T.astype(o_ref.dtype)


def _mla_attn(qm, km, vt):
    B, S, W = qm.shape
    nt = S // T
    return pl.pallas_call(
        _mla_kernel,
        grid=(B, nt),
        in_specs=[pl.BlockSpec((1, T, W), lambda b, i: (b, i, 0)),
                  pl.BlockSpec((1, S, W), lambda b, i: (b, 0, 0)),
                  pl.BlockSpec((1, MLA_HEADS, nt, VT_ROWS, T), lambda b, i: (b, 0, 0, 0, 0))],
        out_specs=pl.BlockSpec((1, T, MLA_HEADS * MLA_V), lambda b, i: (b, i, 0)),
        out_shape=jax.ShapeDtypeStruct((B, S, MLA_HEADS * MLA_V), BF16),
        scratch_shapes=_flash_scratch(MLA_HEADS),
        compiler_params=_cparams(("arbitrary", "arbitrary")),
        name="mla_attn",
    )(qm, km, vt)


def _na_kernel(q_ref, k_ref, v_ref, bias_ref, hmask_ref, o_ref, *, ctx_len, rows):
    i = pl.program_id(2)
    qf = q_ref[0].astype(F32)
    lane_head = hmask_ref[...]
    k_ctx = k_ref[0, 0:ctx_len, :]
    v_ctx = v_ref[0, 0:ctx_len, :]

    @pl.when(i == 0)
    def _():
        out = jnp.zeros((T, LANES), F32)
        for hd in range(2):
            hm = lane_head[hd:hd + 1, :]
            qh = (qf * hm).astype(BF16)
            s = lax.dot_general(qh, k_ctx, _NT, preferred_element_type=F32)
            m = jnp.max(s, axis=1, keepdims=True)
            p = jnp.exp(s - m)
            l = jnp.sum(p, axis=1, keepdims=True)
            o = jnp.dot(p.astype(BF16), v_ctx, preferred_element_type=F32) / l
            out = out + o * hm
        o_ref[0] = out.astype(o_ref.dtype)

    @pl.when(i > 0)
    def _():
        r = (i - 1) * (T // GRID_W)
        start = jnp.clip(r - NA_KH // 2, 0, rows - NA_WIN_ROWS)
        off = pl.multiple_of(ctx_len + start * GRID_W, GRID_W)
        k_win = k_ref[0, pl.ds(off, NA_WIN_ROWS * GRID_W), :]
        v_win = v_ref[0, pl.ds(off, NA_WIN_ROWS * GRID_W), :]
        out = jnp.zeros((T, LANES), F32)
        for hd in range(2):
            hm = lane_head[hd:hd + 1, :]
            qh = (qf * hm).astype(BF16)
            s_w = lax.dot_general(qh, k_win, _NT, preferred_element_type=F32) + bias_ref[0, hd, 0]
            s_c = lax.dot_general(qh, k_ctx, _NT, preferred_element_type=F32)
            m = jnp.maximum(jnp.max(s_w, axis=1, keepdims=True), jnp.max(s_c, axis=1, keepdims=True))
            p_w = jnp.exp(s_w - m)
            p_c = jnp.exp(s_c - m)
            l = jnp.sum(p_w, axis=1, keepdims=True) + jnp.sum(p_c, axis=1, keepdims=True)
            o = (jnp.dot(p_w.astype(BF16), v_win, preferred_element_type=F32)
                 + jnp.dot(p_c.astype(BF16), v_ctx, preferred_element_type=F32)) / l
            out = out + o * hm
        o_ref[0] = out.astype(o_ref.dtype)


def _na_attn(qn, kn, vn, bias_l, hmask, ctx_len):
    B, S, _ = qn.shape
    nt = S // T
    rows = (S - ctx_len) // GRID_W
    G = NA_HEADS // 2
    nlat = nt - ctx_len // T

    def bias_map(b, g, i):
        variant = jnp.where(i <= 1, 0, jnp.where(i == nlat, 2, 1))
        return (0, g, variant, 0, 0)

    return pl.pallas_call(
        functools.partial(_na_kernel, ctx_len=ctx_len, rows=rows),
        grid=(B, G, nt),
        in_specs=[pl.BlockSpec((1, T, LANES), lambda b, g, i: (b, i, g)),
                  pl.BlockSpec((1, S, LANES), lambda b, g, i: (b, 0, g)),
                  pl.BlockSpec((1, S, LANES), lambda b, g, i: (b, 0, g)),
                  pl.BlockSpec((1, 2, 1, T, NA_WIN_ROWS * GRID_W), bias_map),
                  pl.BlockSpec((8, LANES), lambda b, g, i: (0, 0))],
        out_specs=pl.BlockSpec((1, T, LANES), lambda b, g, i: (b, i, g)),
        out_shape=jax.ShapeDtypeStruct((B, S, NA_HEADS * NA_DIM), BF16),
        compiler_params=_cparams(("arbitrary", "arbitrary", "arbitrary")),
        name="na_attn",
    )(qn, kn, vn, bias_l, hmask)


def _outproj_kernel(x_ref, ma_ref, mb_ref, mc_ref, mod_ref, w_ref, g_ref, b_ref, o_ref, *, alpha):
    D = x_ref.shape[-1]
    mix = jnp.concatenate([ma_ref[0], mb_ref[0], mc_ref[0]], axis=1)
    y = jnp.dot(mix, w_ref[...], preferred_element_type=F32)
    gate = mod_ref[0, 0][:, 2 * D:3 * D]
    o_ref[0] = _layernorm(alpha * x_ref[0] + gate * y) * g_ref[...] + b_ref[...]


def _outproj(xs, mix_a, mix_b, mix_c, mod_tok, w_out, g1, b1, alpha):
    B, S, D = xs.shape
    nt = S // T
    tok = lambda w: pl.BlockSpec((1, T, w), lambda b, i: (b, i, 0))
    full = lambda a: pl.BlockSpec(a.shape, lambda b, i: (0,) * a.ndim)
    return pl.pallas_call(
        functools.partial(_outproj_kernel, alpha=alpha),
        grid=(B, nt),
        in_specs=[tok(D), tok(mix_a.shape[-1]), tok(mix_b.shape[-1]), tok(mix_c.shape[-1]),
                  pl.BlockSpec((1, 1, 1, 6 * D), lambda b, i: (b, jnp.minimum(i, 1), 0, 0)),
                  full(w_out), full(g1), full(b1)],
        out_specs=tok(D),
        out_shape=jax.ShapeDtypeStruct((B, S, D), F32),
        compiler_params=_cparams(("arbitrary", "arbitrary")),
        name="outproj",
    )(xs, mix_a, mix_b, mix_c, mod_tok, w_out, g1, b1)


def _ffn_kernel(x_ref, xp_ref, xn_ref, mod_ref, wg_ref, wv_ref, cg_ref, cv_ref, wd_ref, g_ref, b_ref,
                o_ref, acc_ref, *, alpha, nlat_first, nlast):
    D = x_ref.shape[-1]
    i = pl.program_id(1)
    mod = mod_ref[0, 0]
    shift, scale, gate = mod[:, 3 * D:4 * D], mod[:, 4 * D:5 * D], mod[:, 5 * D:6 * D]
    x = x_ref[0]
    has_prev = jnp.logical_and(i != 0, i != nlat_first)
    has_next = jnp.logical_and(i != nlat_first - 1, i != nlast)
    hp = (_layernorm(xp_ref[0]) * (1.0 + scale) + shift) * jnp.where(has_prev, 1.0, 0.0)
    hn = (_layernorm(xn_ref[0]) * (1.0 + scale) + shift) * jnp.where(has_next, 1.0, 0.0)
    hc = _layernorm(x) * (1.0 + scale) + shift
    h = jnp.concatenate([hp, hc, hn], axis=0).astype(BF16)
    n = x.shape[0]
    acc_ref[...] = jnp.zeros_like(acc_ref)

    def conv(u, c):
        return (u[HALO - 1:HALO - 1 + n] * c[0:1] + u[HALO:HALO + n] * c[1:2]
                + u[HALO + 1:HALO + 1 + n] * c[2:3] + c[3:4])

    def body(j, carry):
        ug = conv(jnp.dot(h, wg_ref[j], preferred_element_type=F32), cg_ref[j])
        uv = conv(jnp.dot(h, wv_ref[j], preferred_element_type=F32), cv_ref[j])
        a = ((ug * jax.nn.sigmoid(ug)) * uv).astype(BF16)
        acc_ref[...] += jnp.dot(a, wd_ref[j], preferred_element_type=F32)
        return carry

    lax.fori_loop(0, wg_ref.shape[0], body, 0)
    o_ref[0] = _layernorm(alpha * x + gate * acc_ref[...]) * g_ref[...] + b_ref[...]


def _ffn(xs, mod_tok, wg, wv, cg, cv, wd, g2, b2, alpha, ctx_len):
    B, S, D = xs.shape
    nt = S // T
    th = T // HALO
    nh = S // HALO
    tok = pl.BlockSpec((1, T, D), lambda b, i: (b, i, 0))
    full = lambda a: pl.BlockSpec(a.shape, lambda b, i: (0,) * a.ndim)
    return pl.pallas_call(
        functools.partial(_ffn_kernel, alpha=alpha, nlat_first=ctx_len // T, nlast=nt - 1),
        grid=(B, nt),
        in_specs=[tok,
                  pl.BlockSpec((1, HALO, D), lambda b, i: (b, jnp.maximum(i * th - 1, 0), 0)),
                  pl.BlockSpec((1, HALO, D), lambda b, i: (b, jnp.minimum((i + 1) * th, nh - 1), 0)),
                  pl.BlockSpec((1, 1, 1, 6 * D), lambda b, i: (b, jnp.minimum(i, 1), 0, 0)),
                  full(wg), full(wv), full(cg), full(cv), full(wd), full(g2), full(b2)],
        out_specs=tok,
        out_shape=jax.ShapeDtypeStruct((B, S, D), F32),
        scratch_shapes=[pltpu.VMEM((T, D), F32)],
        compiler_params=_cparams(("arbitrary", "arbitrary")),
        name="ffn",
    )(xs, xs, xs, mod_tok, wg, wv, cg, cv, wd, g2, b2)


def _rot_perm(width):
    j = np.arange(width)
    jj = j % (ROPE_DIM // 2)
    lo = jj < ROPE_DIM // 4
    src = np.where(lo, j + ROPE_DIM // 4, j - ROPE_DIM // 4)
    sign = np.where(lo, -1.0, 1.0).astype(np.float32)
    return src, sign


def _rope_tables(n, ctx_len):
    t = jnp.arange(n)
    row = (t // GRID_W).astype(F32)
    col = (t % GRID_W).astype(F32)
    axis_dim = ROPE_DIM // 2
    inv_freq = ROPE_BASE ** (-jnp.arange(0, axis_dim, 2, dtype=F32) / axis_dim)

    def cs(pos):
        ang = pos[:, None] * inv_freq[None, :]
        ang = jnp.concatenate([ang, ang], axis=-1)
        return jnp.cos(ang), jnp.sin(ang)

    cr, sr = cs(row)
    cc, sc = cs(col)
    cos32 = jnp.concatenate([cr, cc], axis=-1)
    sin32 = jnp.concatenate([sr, sc], axis=-1)
    cos32 = jnp.concatenate([jnp.ones((ctx_len, ROPE_DIM), F32), cos32], axis=0)
    sin32 = jnp.concatenate([jnp.zeros((ctx_len, ROPE_DIM), F32), sin32], axis=0)
    S = n + ctx_len
    cosa = jnp.tile(cos32, (1, LANES // ROPE_DIM))
    sina = jnp.tile(sin32, (1, LANES // ROPE_DIM))
    cosm = jnp.concatenate([jnp.ones((S, MLA_NOPE), F32), cos32, jnp.ones((S, LANES - MLA_NOPE - MLA_ROPE), F32)], axis=1)
    sinm = jnp.concatenate([jnp.zeros((S, MLA_NOPE), F32), sin32, jnp.zeros((S, LANES - MLA_NOPE - MLA_ROPE), F32)], axis=1)
    return cosa, sina, cosm, sinm


def _na_block_index(rows):
    qrows = T // GRID_W
    masked = 2 * NA_KH - 1
    blk = np.full((3, qrows, NA_WIN_ROWS), masked, np.int32)
    for v, r in enumerate((0, 2 * qrows, rows - qrows)):
        start = int(np.clip(r - NA_KH // 2, 0, rows - NA_WIN_ROWS))
        for a in range(qrows):
            qr = r + a
            r0 = int(np.clip(qr - NA_KH // 2, 0, rows - NA_KH))
            for kb in range(NA_WIN_ROWS):
                kr = start + kb
                if r0 <= kr < r0 + NA_KH:
                    blk[v, a, kb] = kr - qr + (NA_KH - 1)
    return blk


def _bias_block_kernel(rpb_ref, dc_ref, band_ref, o_ref):
    rpb = rpb_ref[0]
    dc = dc_ref[...]
    out = jnp.full(o_ref.shape[1:], NEG_INF, F32)
    for d in range(rpb.shape[1]):
        out = jnp.where(dc == d, rpb[:, d:d + 1], out)
    o_ref[0] = jnp.where(band_ref[...] > 0, out, NEG_INF)


def _na_bias_tiles(na_rpb, rows):
    L, H, ndr, ndc = na_rpb.shape
    qc = np.arange(GRID_W)
    c0 = np.clip(qc - NA_KW // 2, 0, GRID_W - NA_KW)
    band = (qc[None, :] >= c0[:, None]) & (qc[None, :] < c0[:, None] + NA_KW)
    dcm = np.clip(qc[None, :] - qc[:, None], -(NA_KW - 1), NA_KW - 1) + (NA_KW - 1)
    ww = GRID_W * GRID_W
    tb = pl.pallas_call(
        _bias_block_kernel,
        grid=(L,),
        in_specs=[pl.BlockSpec((1, H * ndr, ndc), lambda l: (l, 0, 0)),
                  pl.BlockSpec((1, ww), lambda l: (0, 0)),
                  pl.BlockSpec((1, ww), lambda l: (0, 0))],
        out_specs=pl.BlockSpec((1, H * ndr, ww), lambda l: (l, 0, 0)),
        out_shape=jax.ShapeDtypeStruct((L, H * ndr, ww), F32),
        compiler_params=_cparams(("arbitrary",)),
        name="na_bias_blocks",
    )(na_rpb.reshape(L, H * ndr, ndc), jnp.asarray(dcm.reshape(1, ww), jnp.int32),
      jnp.asarray(band.reshape(1, ww), jnp.int32))
    tb = tb.reshape(L, H, ndr, GRID_W, GRID_W)
    tb = jnp.concatenate([tb, jnp.full((L, H, 1, GRID_W, GRID_W), NEG_INF, F32)], axis=2)
    blk = _na_block_index(rows)
    tiles = tb[:, :, blk]
    return tiles.transpose(0, 1, 2, 3, 5, 4, 6).reshape(L, H, 3, T, NA_WIN_ROWS * GRID_W)


def kernel(x, c, ctx, c_ctx, w_ada, b_ada, w_in, lam_q1, lam_k1, lam_q2, lam_k2, diff_norm_w, na_rpb,
           mla_q_norm_w, mla_kv_norm_w, w_uq, w_ukv, w_out, ln1_g, ln1_b, w_up, conv_w, conv_b, w_down,
           ln2_g, ln2_b):
    B, N, D = x.shape
    C = ctx.shape[1]
    L = w_in.shape[0]
    S = C + N
    rows = N // GRID_W
    assert C == T and N % (2 * KT) == 0 and rows >= NA_WIN_ROWS and B + 1 <= 8 and D_FF % FF_CHUNK == 0
    alpha = (2 * L) ** 0.25

    cond8 = jnp.zeros((8, D), F32).at[0:B].set(c).at[B].set(c_ctx)
    mod_all = _ada(cond8, w_ada, b_ada)
    ctx_rows = jnp.broadcast_to(mod_all[:, B][:, None], (L, B, 6 * D))
    mod_tok_all = jnp.stack([ctx_rows, mod_all[:, 0:B]], axis=2).reshape(L, B, 2, 1, 6 * D)

    tabs = _rope_tables(N, C)
    lane = np.arange(LANES)
    qmask = np.zeros((8, LANES), np.float32)
    for s in range(4):
        qmask[s] = (lane // DA_QK == s)
    hmask = np.zeros((8, LANES), np.float32)
    for hd in range(2):
        hmask[hd] = (lane // NA_DIM == hd)
    qmask, hmask = jnp.asarray(qmask), jnp.asarray(hmask)
    bias_all = _na_bias_tiles(na_rpb, rows)

    src_a, sgn_a = _rot_perm(DA_HEADS * 2 * DA_QK)
    src_r, sgn_r = _rot_perm(MLA_ROPE)
    sp = np.cumsum([0, 384, 384, 384, 384, 384, 384, MLA_Q_RANK, MLA_KV_RANK, MLA_ROPE])
    zeros = lambda *s: jnp.zeros(s, F32)

    xs = jnp.concatenate([ctx, x], axis=1)
    for l in range(L):
        wi = w_in[l]
        a_q, a_k, a_v, n_q, n_k, n_v, c_q, c_kv, k_r = [wi[:, sp[t]:sp[t + 1]] for t in range(9)]
        kr_pad = jnp.concatenate([zeros(D, MLA_NOPE), k_r, zeros(D, LANES - MLA_NOPE - MLA_ROPE)], axis=1)
        krr_pad = jnp.concatenate([zeros(D, MLA_NOPE), k_r[:, src_r] * sgn_r,
                                   zeros(D, LANES - MLA_NOPE - MLA_ROPE)], axis=1)
        w_ext = jnp.concatenate([a_q, a_q[:, src_a] * sgn_a, a_k, a_k[:, src_a] * sgn_a, n_q, n_k, n_v,
                                 c_q, c_kv, kr_pad, krr_pad], axis=1).astype(BF16)
        wvt = a_v.T.astype(BF16)
        uq = w_uq[l].reshape(MLA_Q_RANK, MLA_HEADS, MLA_NOPE + MLA_ROPE)
        uq_rope = uq[:, :, MLA_NOPE:]
        padq = jnp.zeros((MLA_Q_RANK, MLA_HEADS, LANES - MLA_NOPE - MLA_ROPE), F32)
        wuq = jnp.concatenate([
            jnp.concatenate([uq, padq], axis=2).reshape(MLA_Q_RANK, MLA_HEADS * LANES),
            jnp.concatenate([jnp.zeros_like(uq[:, :, :MLA_NOPE]), uq_rope[:, :, src_r] * sgn_r, padq],
                            axis=2).reshape(MLA_Q_RANK, MLA_HEADS * LANES)], axis=1).astype(BF16)
        ukv = w_ukv[l].reshape(MLA_KV_RANK, MLA_HEADS, MLA_NOPE + MLA_V)
        wukvk = jnp.concatenate([ukv[:, :, :MLA_NOPE], jnp.zeros((MLA_KV_RANK, MLA_HEADS, LANES - MLA_NOPE), F32)],
                                axis=2).reshape(MLA_KV_RANK, MLA_HEADS * LANES).astype(BF16)
        wukvvt = ukv[:, :, MLA_NOPE:].reshape(MLA_KV_RANK, MLA_HEADS * MLA_V).T.astype(BF16)
        gq = mla_q_norm_w[l].reshape(1, MLA_Q_RANK)
        gkv = mla_kv_norm_w[l].reshape(1, MLA_KV_RANK)
        mod_tok = mod_tok_all[l]

        qa, ka, qn, kn, vn, qm, km, vt = _inproj(xs, mod_tok, tabs, (w_ext, wvt, gq, gkv, wuq, wukvk, wukvvt))

        lam_init = 0.8 - 0.6 * math.exp(-0.3 * l)
        lamp = jnp.zeros((1, 8, DA_QK), F32)
        lamp = lamp.at[0, 0].set(lam_q1[l]).at[0, 1].set(lam_k1[l]).at[0, 2].set(lam_q2[l]).at[0, 3].set(lam_k2[l])
        lamp = lamp.at[0, 4].set(lam_init)
        dnw = jnp.concatenate([diff_norm_w[l], diff_norm_w[l]]).reshape(1, 1, LANES)
        mix_a = _diff_attn(qa, ka, vt, qmask, lamp, dnw)

        mix_b = _na_attn(qn, kn, vn, bias_all[l:l + 1], hmask, C)
        mix_c = _mla_attn(qm, km, vt)

        x1 = _outproj(xs, mix_a, mix_b, mix_c, mod_tok, w_out[l].astype(BF16),
                      ln1_g[l].reshape(1, D), ln1_b[l].reshape(1, D), alpha)

        nck = D_FF // FF_CHUNK
        wu = w_up[l].astype(BF16)
        wg = wu[:, :D_FF].reshape(D, nck, FF_CHUNK).transpose(1, 0, 2)
        wv = wu[:, D_FF:].reshape(D, nck, FF_CHUNK).transpose(1, 0, 2)
        cw = jnp.concatenate([conv_w[l], conv_b[l][None], jnp.zeros((4, 2 * D_FF), F32)], axis=0)
        cg = cw[:, :D_FF].reshape(8, nck, FF_CHUNK).transpose(1, 0, 2)
        cv = cw[:, D_FF:].reshape(8, nck, FF_CHUNK).transpose(1, 0, 2)
        wd = w_down[l].astype(BF16).reshape(nck, FF_CHUNK, D)
        xs = _ffn(x1, mod_tok, wg, wv, cg, cv, wd, ln2_g[l].reshape(1, D), ln2_b[l].reshape(1, D), alpha, C)
    return xs[:, C:]
```

```python
import functools
import math

import numpy as np
import jax
import jax.numpy as jnp
from jax import lax
from jax.experimental import pallas as pl
from jax.experimental.pallas import tpu as pltpu

F32 = jnp.float32
BF16 = jnp.bfloat16

GRID_W = 64
ROPE_DIM = 32
ROPE_BASE = 10000.0
DA_HEADS, DA_QK, DA_V = 6, 32, 64
NA_HEADS, NA_DIM, NA_KH, NA_KW = 6, 64, 8, 16
MLA_HEADS, MLA_Q_RANK, MLA_KV_RANK, MLA_NOPE, MLA_ROPE, MLA_V = 4, 256, 128, 64, 32, 64
D_FF = 2816
LN_EPS = 1e-6
NEG_INF = -1e30
DA_SCALE = DA_QK ** -0.5
NA_SCALE = NA_DIM ** -0.5
MLA_SCALE = (MLA_NOPE + MLA_ROPE) ** -0.5
LOG2E = math.log2(math.e)

T = 256
KT = 3 * T
LANES = 128
VT_ROWS = 80
FF_CHUNK = D_FF // 2
NA_WIN_ROWS = 12
HALO = 8
VMEM_LIMIT = 56 * 1024 * 1024

_NT = (((1,), (1,)), ((), ()))


def _layernorm(x):
    mu = jnp.mean(x, axis=-1, keepdims=True)
    xc = x - mu
    var = jnp.mean(xc * xc, axis=-1, keepdims=True)
    return xc * lax.rsqrt(var + LN_EPS)


def _rms(x, g):
    return (x * lax.rsqrt(jnp.mean(x * x, axis=-1, keepdims=True) + LN_EPS)) * g


def _cparams(sem):
    return pltpu.CompilerParams(dimension_semantics=sem, vmem_limit_bytes=VMEM_LIMIT)


def _ada_kernel(cond_ref, w_ref, b_ref, o_ref):
    c = cond_ref[...]
    a = (c * jax.nn.sigmoid(c)).astype(BF16)
    o_ref[0] = jnp.dot(a, w_ref[0].astype(BF16), preferred_element_type=F32) + b_ref[0]


def _ada(cond8, w_ada, b_ada):
    L, D, N = w_ada.shape
    tn = 1536
    return pl.pallas_call(
        _ada_kernel,
        grid=(L, N // tn),
        in_specs=[pl.BlockSpec((8, D), lambda l, j: (0, 0)),
                  pl.BlockSpec((1, D, tn), lambda l, j: (l, 0, j)),
                  pl.BlockSpec((1, 1, tn), lambda l, j: (l, 0, j))],
        out_specs=pl.BlockSpec((1, 8, tn), lambda l, j: (l, 0, j)),
        out_shape=jax.ShapeDtypeStruct((L, 8, N), F32),
        compiler_params=_cparams(("arbitrary", "arbitrary")),
        name="ada",
    )(cond8, w_ada, b_ada.reshape(L, 1, N))


def _inproj_kernel(x_ref, mod_ref, cosa_ref, sina_ref, cosm_ref, sinm_ref, w_ref, wvt_ref,
                   gq_ref, gkv_ref, wuq_ref, wukvk_ref, wukvvt_ref,
                   qa_ref, ka_ref, qn_ref, kn_ref, vn_ref, qm_ref, km_ref, vt_ref):
    D = x_ref.shape[-1]
    x = x_ref[0]
    mod = mod_ref[0, 0]
    h = (_layernorm(x) * (1.0 + mod[:, D:2 * D]) + mod[:, 0:D]).astype(BF16)
    proj = jnp.dot(h, w_ref[...], preferred_element_type=F32)

    cosa, sina = cosa_ref[...], sina_ref[...]
    cos3 = jnp.concatenate([cosa] * 3, axis=1)
    sin3 = jnp.concatenate([sina] * 3, axis=1)
    qa_ref[0] = ((proj[:, 0:384] * cos3 + proj[:, 384:768] * sin3) * (DA_SCALE * LOG2E)).astype(BF16)
    ka_ref[0] = (proj[:, 768:1152] * cos3 + proj[:, 1152:1536] * sin3).astype(BF16)
    qn_ref[0] = (proj[:, 1536:1920] * NA_SCALE).astype(BF16)
    kn_ref[0] = proj[:, 1920:2304].astype(BF16)
    vn_ref[0] = proj[:, 2304:2688].astype(BF16)

    cosm, sinm = cosm_ref[...], sinm_ref[...]
    cos4 = jnp.concatenate([cosm] * 4, axis=1)
    sin4 = jnp.concatenate([sinm] * 4, axis=1)
    cqn = _rms(proj[:, 2688:2944], gq_ref[...]).astype(BF16)
    qc = jnp.dot(cqn, wuq_ref[...], preferred_element_type=F32)
    qm_ref[0] = ((qc[:, 0:512] * cos4 + qc[:, 512:1024] * sin4) * (MLA_SCALE * LOG2E)).astype(BF16)
    kvn = _rms(proj[:, 2944:3072], gkv_ref[...]).astype(BF16)
    kr = proj[:, 3072:3200] * cosm + proj[:, 3200:3328] * sinm
    km = jnp.dot(kvn, wukvk_ref[...], preferred_element_type=F32) + jnp.concatenate([kr] * 4, axis=1)
    km_ref[0] = km.astype(BF16)

    vta = lax.dot_general(wvt_ref[...], h, _NT, preferred_element_type=F32)
    vtm = lax.dot_general(wukvvt_ref[...], kvn, _NT, preferred_element_type=F32)
    ones = jnp.ones((VT_ROWS - DA_V, x.shape[0]), BF16)
    for hd in range(MLA_HEADS):
        vt_ref[0, hd, 0, 0:MLA_V, :] = vtm[hd * MLA_V:(hd + 1) * MLA_V].astype(BF16)
        vt_ref[0, hd, 0, MLA_V:VT_ROWS, :] = ones
    for hd in range(DA_HEADS):
        vt_ref[0, MLA_HEADS + hd, 0, 0:DA_V, :] = vta[hd * DA_V:(hd + 1) * DA_V].astype(BF16)
        vt_ref[0, MLA_HEADS + hd, 0, DA_V:VT_ROWS, :] = ones


def _inproj(xs, mod_tok, tabs, wts):
    B, S, D = xs.shape
    nt = S // T
    cosa, sina, cosm, sinm = tabs
    w_ext, wvt, gq, gkv, wuq, wukvk, wukvvt = wts
    tok = lambda w: pl.BlockSpec((1, T, w), lambda b, i: (b, i, 0))
    tab = pl.BlockSpec((T, LANES), lambda b, i: (i, 0))
    full = lambda a: pl.BlockSpec(a.shape, lambda b, i: (0,) * a.ndim)
    nvt = DA_HEADS + MLA_HEADS
    out_shape = [jax.ShapeDtypeStruct((B, S, w), BF16) for w in (384, 384, 384, 384, 384, 512, 512)]
    out_shape.append(jax.ShapeDtypeStruct((B, nvt, nt, VT_ROWS, T), BF16))
    out_specs = [tok(w) for w in (384, 384, 384, 384, 384, 512, 512)]
    out_specs.append(pl.BlockSpec((1, nvt, 1, VT_ROWS, T), lambda b, i: (b, 0, i, 0, 0)))
    return pl.pallas_call(
        _inproj_kernel,
        grid=(B, nt),
        in_specs=[tok(D),
                  pl.BlockSpec((1, 1, 1, 6 * D), lambda b, i: (b, jnp.minimum(i, 1), 0, 0)),
                  tab, tab, tab, tab,
                  full(w_ext), full(wvt), full(gq), full(gkv), full(wuq), full(wukvk), full(wukvvt)],
        out_specs=out_specs,
        out_shape=out_shape,
        compiler_params=_cparams(("arbitrary", "arbitrary")),
        name="inproj",
    )(xs, mod_tok, cosa, sina, cosm, sinm, w_ext, wvt, gq, gkv, wuq, wukvk, wukvvt)


def _flash_t(qts, k_rows_fn, vt_tile_fn, ctx_only, ntiles, scratch):
    s_sc, cm_sc = scratch[0:2], scratch[2:4]
    m_sc, acc_sc = scratch[4:6]
    nsub = len(qts)
    nchunk = KT // T

    def cols(s):
        return slice(s * T, (s + 1) * T)

    def scores_chunk(start, nrows, s):
        return jnp.dot(k_rows_fn(start, nrows, s), qts[s], preferred_element_type=F32)

    def half(u_next, u_cur, r):
        w = 1 - r
        for c in range(nchunk):
            rows = slice(c * T, (c + 1) * T)
            if u_next is not None:
                start = pl.multiple_of(u_next * KT + c * T, T)
                for s in range(nsub):
                    st = scores_chunk(start, T, s)
                    s_sc[w][rows, cols(s)] = st
                    cm_sc[w][c:c + 1, cols(s)] = jnp.max(st, axis=0, keepdims=True)
            if u_cur is not None:
                m_old = m_sc[...]
                m_new = jnp.maximum(m_old, cm_sc[r][c:c + 1, :])
                alpha = jnp.exp2(m_old - m_new)
                m_sc[...] = m_new
                p = jnp.exp2(s_sc[r][rows, :] - m_new).astype(BF16)
                for s in range(nsub):
                    pv = jnp.dot(vt_tile_fn(nchunk * u_cur + c, s), p[:, cols(s)], preferred_element_type=F32)
                    acc_sc[:, cols(s)] = alpha[:, cols(s)] * acc_sc[:, cols(s)] + pv

    @pl.when(ctx_only)
    def _():
        for s in range(nsub):
            st = scores_chunk(0, T, s)
            p = jnp.exp2(st - jnp.max(st, axis=0, keepdims=True)).astype(BF16)
            acc_sc[:, cols(s)] = jnp.dot(vt_tile_fn(0, s), p, preferred_element_type=F32)

    @pl.when(jnp.logical_not(ctx_only))
    def _():
        m_sc[...] = jnp.full(m_sc.shape, NEG_INF, F32)
        acc_sc[...] = jnp.zeros(acc_sc.shape, F32)
        half(0, None, 1)

        def body(q, carry):
            half(2 * q + 1, 2 * q, 0)
            half(2 * q + 2, 2 * q + 1, 1)
            return carry

        lax.fori_loop(0, (ntiles - 1) // 2, body, 0)
        half(None, ntiles - 1, 0)

    acc = acc_sc[...]
    return [acc[0:64, cols(s)] / acc[64:65, cols(s)] for s in range(nsub)]


def _flash_scratch(nsub):
    w = nsub * T
    big, row = (KT, w), (1, w)
    return [pltpu.VMEM(big, F32), pltpu.VMEM(big, F32),
            pltpu.VMEM((8, w), F32), pltpu.VMEM((8, w), F32),
            pltpu.VMEM(row, F32), pltpu.VMEM((VT_ROWS, w), F32)]


def _diff_kernel(q_ref, k_ref, vt_ref, qmask_ref, lamp_ref, dnw_ref, o_ref, *scratch):
    ctx_only = pl.program_id(2) == 0
    ntiles = k_ref.shape[1] // KT
    qf = q_ref[0].astype(F32)
    qts = [(qf * qmask_ref[s:s + 1, :]).T.astype(BF16) for s in range(4)]

    def k_rows(start, n, s):
        return k_ref[0, pl.ds(start, n), :]

    def vt_tile(j, s):
        return vt_ref[0, s // 2, j]

    outs = _flash_t(qts, k_rows, vt_tile, ctx_only, ntiles, scratch)
    lp = lamp_ref[0]
    lam_init = lp[4:5, 0:1]
    lam = (jnp.exp(jnp.sum(lp[0:1] * lp[1:2], axis=1, keepdims=True))
           - jnp.exp(jnp.sum(lp[2:3] * lp[3:4], axis=1, keepdims=True)) + lam_init)
    ys = []
    for hd in range(2):
        o = outs[2 * hd] - lam * outs[2 * hd + 1]
        ys.append(o * lax.rsqrt(jnp.mean(o * o, axis=0, keepdims=True) + LN_EPS))
    y = jnp.concatenate(ys, axis=0).T
    o_ref[0] = ((y * dnw_ref[0]) * (1.0 - lam_init)).astype(o_ref.dtype)


def _diff_attn(qa, ka, vt, qmask, lamp_l, dnw_l):
    B, S, _ = qa.shape
    nt = S // T
    G = DA_HEADS // 2
    return pl.pallas_call(
        _diff_kernel,
        grid=(B, G, nt),
        in_specs=[pl.BlockSpec((1, T, LANES), lambda b, g, i: (b, i, g)),
                  pl.BlockSpec((1, S, LANES), lambda b, g, i: (b, 0, g)),
                  pl.BlockSpec((1, 2, nt, VT_ROWS, T), lambda b, g, i: (b, MLA_HEADS // 2 + g, 0, 0, 0)),
                  pl.BlockSpec((8, LANES), lambda b, g, i: (0, 0)),
                  pl.BlockSpec((1, 8, DA_QK), lambda b, g, i: (0, 0, 0)),
                  pl.BlockSpec((1, 1, LANES), lambda b, g, i: (0, 0, 0))],
        out_specs=pl.BlockSpec((1, T, LANES), lambda b, g, i: (b, i, g)),
        out_shape=jax.ShapeDtypeStruct((B, S, DA_HEADS * DA_V), BF16),
        scratch_shapes=_flash_scratch(4),
        compiler_params=_cparams(("arbitrary", "arbitrary", "arbitrary")),
        name="diff_attn",
    )(qa, ka, vt, qmask, lamp_l, dnw_l)


def _mla_kernel(q_ref, k_ref, vt_ref, o_ref, *scratch):
    ctx_only = pl.program_id(1) == 0
    ntiles = k_ref.shape[1] // KT
    qf = q_ref[0].astype(F32)
    qts = [qf[:, s * LANES:(s + 1) * LANES].T.astype(BF16) for s in range(MLA_HEADS)]

    def k_rows(start, n, s):
        return k_ref[0, pl.ds(start, n), s * LANES:(s + 1) * LANES]

    def vt_tile(j, s):
        return vt_ref[0, s, j]

    outs = _flash_t(qts, k_rows, vt_tile, ctx_only, ntiles, scratch)
    o_ref[0] = jnp.concatenate(outs, axis=0).T.astype(o_ref.dtype)


def _mla_attn(qm, km, vt):
    B, S, W = qm.shape
    nt = S // T
    return pl.pallas_call(
        _mla_kernel,
        grid=(B, nt),
        in_specs=[pl.BlockSpec((1, T, W), lambda b, i: (b, i, 0)),
                  pl.BlockSpec((1, S, W), lambda b, i: (b, 0, 0)),
                  pl.BlockSpec((1, MLA_HEADS, nt, VT_ROWS, T), lambda b, i: (b, 0, 0, 0, 0))],
        out_specs=pl.BlockSpec((1, T, MLA_HEADS * MLA_V), lambda b, i: (b, i, 0)),
        out_shape=jax.ShapeDtypeStruct((B, S, MLA_HEADS * MLA_V), BF16),
        scratch_shapes=_flash_scratch(MLA_HEADS),
        compiler_params=_cparams(("arbitrary", "arbitrary")),
        name="mla_attn",
    )(qm, km, vt)


def _na_kernel(q_ref, k_ref, v_ref, bias_ref, hmask_ref, o_ref, *, ctx_len, rows):
    i = pl.program_id(2)
    qf = q_ref[0].astype(F32)
    lane_head = hmask_ref[...]
    k_ctx = k_ref[0, 0:ctx_len, :]
    v_ctx = v_ref[0, 0:ctx_len, :]

    @pl.when(i == 0)
    def _():
        out = jnp.zeros((T, LANES), F32)
        for hd in range(2):
            hm = lane_head[hd:hd + 1, :]
            qh = (qf * hm).astype(BF16)
            s = lax.dot_general(qh, k_ctx, _NT, preferred_element_type=F32)
            m = jnp.max(s, axis=1, keepdims=True)
            p = jnp.exp(s - m)
            l = jnp.sum(p, axis=1, keepdims=True)
            o = jnp.dot(p.astype(BF16), v_ctx, preferred_element_type=F32) / l
            out = out + o * hm
        o_ref[0] = out.astype(o_ref.dtype)

    @pl.when(i > 0)
    def _():
        r = (i - 1) * (T // GRID_W)
        start = jnp.clip(r - NA_KH // 2, 0, rows - NA_WIN_ROWS)
        off = pl.multiple_of(ctx_len + start * GRID_W, GRID_W)
        k_win = k_ref[0, pl.ds(off, NA_WIN_ROWS * GRID_W), :]
        v_win = v_ref[0, pl.ds(off, NA_WIN_ROWS * GRID_W), :]
        out = jnp.zeros((T, LANES), F32)
        for hd in range(2):
            hm = lane_head[hd:hd + 1, :]
            qh = (qf * hm).astype(BF16)
            s_w = lax.dot_general(qh, k_win, _NT, preferred_element_type=F32) + bias_ref[0, hd, 0]
            s_c = lax.dot_general(qh, k_ctx, _NT, preferred_element_type=F32)
            m = jnp.maximum(jnp.max(s_w, axis=1, keepdims=True), jnp.max(s_c, axis=1, keepdims=True))
            p_w = jnp.exp(s_w - m)
            p_c = jnp.exp(s_c - m)
            l = jnp.sum(p_w, axis=1, keepdims=True) + jnp.sum(p_c, axis=1, keepdims=True)
            o = (jnp.dot(p_w.astype(BF16), v_win, preferred_element_type=F32)
                 + jnp.dot(p_c.astype(BF16), v_ctx, preferred_element_type=F32)) / l
            out = out + o * hm
        o_ref[0] = out.astype(o_ref.dtype)


def _na_attn(qn, kn, vn, bias_l, hmask, ctx_len):
    B, S, _ = qn.shape
    nt = S // T
    rows = (S - ctx_len) // GRID_W
    G = NA_HEADS // 2
    nlat = nt - ctx_len // T

    def bias_map(b, g, i):
        variant = jnp.where(i <= 1, 0, jnp.where(i == nlat, 2, 1))
        return (0, g, variant, 0, 0)

    return pl.pallas_call(
        functools.partial(_na_kernel, ctx_len=ctx_len, rows=rows),
        grid=(B, G, nt),
        in_specs=[pl.BlockSpec((1, T, LANES), lambda b, g, i: (b, i, g)),
                  pl.BlockSpec((1, S, LANES), lambda b, g, i: (b, 0, g)),
                  pl.BlockSpec((1, S, LANES), lambda b, g, i: (b, 0, g)),
                  pl.BlockSpec((1, 2, 1, T, NA_WIN_ROWS * GRID_W), bias_map),
                  pl.BlockSpec((8, LANES), lambda b, g, i: (0, 0))],
        out_specs=pl.BlockSpec((1, T, LANES), lambda b, g, i: (b, i, g)),
        out_shape=jax.ShapeDtypeStruct((B, S, NA_HEADS * NA_DIM), BF16),
        compiler_params=_cparams(("arbitrary", "arbitrary", "arbitrary")),
        name="na_attn",
    )(qn, kn, vn, bias_l, hmask)


def _outproj_kernel(x_ref, ma_ref, mb_ref, mc_ref, mod_ref, w_ref, g_ref, b_ref, o_ref, *, alpha):
    D = x_ref.shape[-1]
    mix = jnp.concatenate([ma_ref[0], mb_ref[0], mc_ref[0]], axis=1)
    y = jnp.dot(mix, w_ref[...], preferred_element_type=F32)
    gate = mod_ref[0, 0][:, 2 * D:3 * D]
    o_ref[0] = _layernorm(alpha * x_ref[0] + gate * y) * g_ref[...] + b_ref[...]


def _outproj(xs, mix_a, mix_b, mix_c, mod_tok, w_out, g1, b1, alpha):
    B, S, D = xs.shape
    nt = S // T
    tok = lambda w: pl.BlockSpec((1, T, w), lambda b, i: (b, i, 0))
    full = lambda a: pl.BlockSpec(a.shape, lambda b, i: (0,) * a.ndim)
    return pl.pallas_call(
        functools.partial(_outproj_kernel, alpha=alpha),
        grid=(B, nt),
        in_specs=[tok(D), tok(mix_a.shape[-1]), tok(mix_b.shape[-1]), tok(mix_c.shape[-1]),
                  pl.BlockSpec((1, 1, 1, 6 * D), lambda b, i: (b, jnp.minimum(i, 1), 0, 0)),
                  full(w_out), full(g1), full(b1)],
        out_specs=tok(D),
        out_shape=jax.ShapeDtypeStruct((B, S, D), F32),
        compiler_params=_cparams(("arbitrary", "arbitrary")),
        name="outproj",
    )(xs, mix_a, mix_b, mix_c, mod_tok, w_out, g1, b1)


def _ffn_kernel(x_ref, xp_ref, xn_ref, mod_ref, wg_ref, wv_ref, cg_ref, cv_ref, wd_ref, g_ref, b_ref,
                o_ref, *u_refs, alpha, nlat_first, nlast):
    D = x_ref.shape[-1]
    i = pl.program_id(1)
    mod = mod_ref[0, 0]
    shift, scale, gate = mod[:, 3 * D:4 * D], mod[:, 4 * D:5 * D], mod[:, 5 * D:6 * D]
    x = x_ref[0]
    has_prev = jnp.logical_and(i != 0, i != nlat_first)
    has_next = jnp.logical_and(i != nlat_first - 1, i != nlast)
    hp = (_layernorm(xp_ref[0]) * (1.0 + scale) + shift) * jnp.where(has_prev, 1.0, 0.0)
    hn = (_layernorm(xn_ref[0]) * (1.0 + scale) + shift) * jnp.where(has_next, 1.0, 0.0)
    hc = _layernorm(x) * (1.0 + scale) + shift
    h = jnp.concatenate([hp, hc, hn], axis=0).astype(BF16)
    n = x.shape[0]
    nck = wg_ref.shape[0]

    for j in range(nck):
        u_refs[2 * j][...] = jnp.dot(h, wg_ref[j], preferred_element_type=F32)
        u_refs[2 * j + 1][...] = jnp.dot(h, wv_ref[j], preferred_element_type=F32)

    def conv(u_ref, c):
        return (u_ref[HALO - 1:HALO - 1 + n, :] * c[0:1] + u_ref[HALO:HALO + n, :] * c[1:2]
                + u_ref[HALO + 1:HALO + 1 + n, :] * c[2:3] + c[3:4])

    y = None
    for j in range(nck):
        ug = conv(u_refs[2 * j], cg_ref[j])
        uv = conv(u_refs[2 * j + 1], cv_ref[j])
        a = ((ug * jax.nn.sigmoid(ug)) * uv).astype(BF16)
        d = jnp.dot(a, wd_ref[j], preferred_element_type=F32)
        y = d if y is None else y + d
    o_ref[0] = _layernorm(alpha * x + gate * y) * g_ref[...] + b_ref[...]


def _ffn(xs, mod_tok, wg, wv, cg, cv, wd, g2, b2, alpha, ctx_len):
    B, S, D = xs.shape
    nt = S // T
    th = T // HALO
    nh = S // HALO
    tok = pl.BlockSpec((1, T, D), lambda b, i: (b, i, 0))
    full = lambda a: pl.BlockSpec(a.shape, lambda b, i: (0,) * a.ndim)
    return pl.pallas_call(
        functools.partial(_ffn_kernel, alpha=alpha, nlat_first=ctx_len // T, nlast=nt - 1),
        grid=(B, nt),
        in_specs=[tok,
                  pl.BlockSpec((1, HALO, D), lambda b, i: (b, jnp.maximum(i * th - 1, 0), 0)),
                  pl.BlockSpec((1, HALO, D), lambda b, i: (b, jnp.minimum((i + 1) * th, nh - 1), 0)),
                  pl.BlockSpec((1, 1, 1, 6 * D), lambda b, i: (b, jnp.minimum(i, 1), 0, 0)),
                  full(wg), full(wv), full(cg), full(cv), full(wd), full(g2), full(b2)],
        out_specs=tok,
        out_shape=jax.ShapeDtypeStruct((B, S, D), F32),
        scratch_shapes=[pltpu.VMEM((T + 2 * HALO, wg.shape[-1]), F32) for _ in range(2 * wg.shape[0])],
        compiler_params=_cparams(("arbitrary", "arbitrary")),
        name="ffn",
    )(xs, xs, xs, mod_tok, wg, wv, cg, cv, wd, g2, b2)


def _rot_perm(width):
    j = np.arange(width)
    jj = j % (ROPE_DIM // 2)
    lo = jj < ROPE_DIM // 4
    src = np.where(lo, j + ROPE_DIM // 4, j - ROPE_DIM // 4)
    sign = np.where(lo, -1.0, 1.0).astype(np.float32)
    return src, sign


def _rope_tables(n, ctx_len):
    t = jnp.arange(n)
    row = (t // GRID_W).astype(F32)
    col = (t % GRID_W).astype(F32)
    axis_dim = ROPE_DIM // 2
    inv_freq = ROPE_BASE ** (-jnp.arange(0, axis_dim, 2, dtype=F32) / axis_dim)

    def cs(pos):
        ang = pos[:, None] * inv_freq[None, :]
        ang = jnp.concatenate([ang, ang], axis=-1)
        return jnp.cos(ang), jnp.sin(ang)

    cr, sr = cs(row)
    cc, sc = cs(col)
    cos32 = jnp.concatenate([cr, cc], axis=-1)
    sin32 = jnp.concatenate([sr, sc], axis=-1)
    cos32 = jnp.concatenate([jnp.ones((ctx_len, ROPE_DIM), F32), cos32], axis=0)
    sin32 = jnp.concatenate([jnp.zeros((ctx_len, ROPE_DIM), F32), sin32], axis=0)
    S = n + ctx_len
    cosa = jnp.tile(cos32, (1, LANES // ROPE_DIM))
    sina = jnp.tile(sin32, (1, LANES // ROPE_DIM))
    cosm = jnp.concatenate([jnp.ones((S, MLA_NOPE), F32), cos32, jnp.ones((S, LANES - MLA_NOPE - MLA_ROPE), F32)], axis=1)
    sinm = jnp.concatenate([jnp.zeros((S, MLA_NOPE), F32), sin32, jnp.zeros((S, LANES - MLA_NOPE - MLA_ROPE), F32)], axis=1)
    return cosa, sina, cosm, sinm


def _na_block_index(rows):
    qrows = T // GRID_W
    masked = 2 * NA_KH - 1
    blk = np.full((3, qrows, NA_WIN_ROWS), masked, np.int32)
    for v, r in enumerate((0, 2 * qrows, rows - qrows)):
        start = int(np.clip(r - NA_KH // 2, 0, rows - NA_WIN_ROWS))
        for a in range(qrows):
            qr = r + a
            r0 = int(np.clip(qr - NA_KH // 2, 0, rows - NA_KH))
            for kb in range(NA_WIN_ROWS):
                kr = start + kb
                if r0 <= kr < r0 + NA_KH:
                    blk[v, a, kb] = kr - qr + (NA_KH - 1)
    return blk


def _bias_block_kernel(rpb_ref, dc_ref, band_ref, o_ref):
    rpb = rpb_ref[0]
    dc = dc_ref[...]
    out = jnp.full(o_ref.shape[1:], NEG_INF, F32)
    for d in range(rpb.shape[1]):
        out = jnp.where(dc == d, rpb[:, d:d + 1], out)
    o_ref[0] = jnp.where(band_ref[...] > 0, out, NEG_INF)


def _na_bias_tiles(na_rpb, rows):
    L, H, ndr, ndc = na_rpb.shape
    qc = np.arange(GRID_W)
    c0 = np.clip(qc - NA_KW // 2, 0, GRID_W - NA_KW)
    band = (qc[None, :] >= c0[:, None]) & (qc[None, :] < c0[:, None] + NA_KW)
    dcm = np.clip(qc[None, :] - qc[:, None], -(NA_KW - 1), NA_KW - 1) + (NA_KW - 1)
    ww = GRID_W * GRID_W
    tb = pl.pallas_call(
        _bias_block_kernel,
        grid=(L,),
        in_specs=[pl.BlockSpec((1, H * ndr, ndc), lambda l: (l, 0, 0)),
                  pl.BlockSpec((1, ww), lambda l: (0, 0)),
                  pl.BlockSpec((1, ww), lambda l: (0, 0))],
        out_specs=pl.BlockSpec((1, H * ndr, ww), lambda l: (l, 0, 0)),
        out_shape=jax.ShapeDtypeStruct((L, H * ndr, ww), F32),
        compiler_params=_cparams(("arbitrary",)),
        name="na_bias_blocks",
    )(na_rpb.reshape(L, H * ndr, ndc), jnp.asarray(dcm.reshape(1, ww), jnp.int32),
      jnp.asarray(band.reshape(1, ww), jnp.int32))
    tb = tb.reshape(L, H, ndr, GRID_W, GRID_W)
    tb = jnp.concatenate([tb, jnp.full((L, H, 1, GRID_W, GRID_W), NEG_INF, F32)], axis=2)
    blk = _na_block_index(rows)
    tiles = tb[:, :, blk]
    return tiles.transpose(0, 1, 2, 3, 5, 4, 6).reshape(L, H, 3, T, NA_WIN_ROWS * GRID_W)


def kernel(x, c, ctx, c_ctx, w_ada, b_ada, w_in, lam_q1, lam_k1, lam_q2, lam_k2, diff_norm_w, na_rpb,
           mla_q_norm_w, mla_kv_norm_w, w_uq, w_ukv, w_out, ln1_g, ln1_b, w_up, conv_w, conv_b, w_down,
           ln2_g, ln2_b):
    B, N, D = x.shape
    C = ctx.shape[1]
    L = w_in.shape[0]
    S = C + N
    rows = N // GRID_W
    assert C == T and N % T == 0 and S % KT == 0 and (S // KT) % 2 == 1
    assert rows >= NA_WIN_ROWS and B + 1 <= 8 and D_FF % FF_CHUNK == 0
    alpha = (2 * L) ** 0.25

    cond8 = jnp.zeros((8, D), F32).at[0:B].set(c).at[B].set(c_ctx)
    mod_all = _ada(cond8, w_ada, b_ada)
    ctx_rows = jnp.broadcast_to(mod_all[:, B][:, None], (L, B, 6 * D))
    mod_tok_all = jnp.stack([ctx_rows, mod_all[:, 0:B]], axis=2).reshape(L, B, 2, 1, 6 * D)

    tabs = _rope_tables(N, C)
    lane = np.arange(LANES)
    qmask = np.zeros((8, LANES), np.float32)
    for s in range(4):
        qmask[s] = (lane // DA_QK == s)
    hmask = np.zeros((8, LANES), np.float32)
    for hd in range(2):
        hmask[hd] = (lane // NA_DIM == hd)
    qmask, hmask = jnp.asarray(qmask), jnp.asarray(hmask)
    bias_all = _na_bias_tiles(na_rpb, rows)

    src_a, sgn_a = _rot_perm(DA_HEADS * 2 * DA_QK)
    src_r, sgn_r = _rot_perm(MLA_ROPE)
    sp = np.cumsum([0, 384, 384, 384, 384, 384, 384, MLA_Q_RANK, MLA_KV_RANK, MLA_ROPE])
    zeros = lambda *s: jnp.zeros(s, F32)

    xs = jnp.concatenate([ctx, x], axis=1)
    for l in range(L):
        wi = w_in[l]
        a_q, a_k, a_v, n_q, n_k, n_v, c_q, c_kv, k_r = [wi[:, sp[t]:sp[t + 1]] for t in range(9)]
        kr_pad = jnp.concatenate([zeros(D, MLA_NOPE), k_r, zeros(D, LANES - MLA_NOPE - MLA_ROPE)], axis=1)
        krr_pad = jnp.concatenate([zeros(D, MLA_NOPE), k_r[:, src_r] * sgn_r,
                                   zeros(D, LANES - MLA_NOPE - MLA_ROPE)], axis=1)
        w_ext = jnp.concatenate([a_q, a_q[:, src_a] * sgn_a, a_k, a_k[:, src_a] * sgn_a, n_q, n_k, n_v,
                                 c_q, c_kv, kr_pad, krr_pad], axis=1).astype(BF16)
        wvt = a_v.T.astype(BF16)
        uq = w_uq[l].reshape(MLA_Q_RANK, MLA_HEADS, MLA_NOPE + MLA_ROPE)
        uq_rope = uq[:, :, MLA_NOPE:]
        padq = jnp.zeros((MLA_Q_RANK, MLA_HEADS, LANES - MLA_NOPE - MLA_ROPE), F32)
        wuq = jnp.concatenate([
            jnp.concatenate([uq, padq], axis=2).reshape(MLA_Q_RANK, MLA_HEADS * LANES),
            jnp.concatenate([jnp.zeros_like(uq[:, :, :MLA_NOPE]), uq_rope[:, :, src_r] * sgn_r, padq],
                            axis=2).reshape(MLA_Q_RANK, MLA_HEADS * LANES)], axis=1).astype(BF16)
        ukv = w_ukv[l].reshape(MLA_KV_RANK, MLA_HEADS, MLA_NOPE + MLA_V)
        wukvk = jnp.concatenate([ukv[:, :, :MLA_NOPE], jnp.zeros((MLA_KV_RANK, MLA_HEADS, LANES - MLA_NOPE), F32)],
                                axis=2).reshape(MLA_KV_RANK, MLA_HEADS * LANES).astype(BF16)
        wukvvt = ukv[:, :, MLA_NOPE:].reshape(MLA_KV_RANK, MLA_HEADS * MLA_V).T.astype(BF16)
        gq = mla_q_norm_w[l].reshape(1, MLA_Q_RANK)
        gkv = mla_kv_norm_w[l].reshape(1, MLA_KV_RANK)
        mod_tok = mod_tok_all[l]

        qa, ka, qn, kn, vn, qm, km, vt = _inproj(xs, mod_tok, tabs, (w_ext, wvt, gq, gkv, wuq, wukvk, wukvvt))

        lam_init = 0.8 - 0.6 * math.exp(-0.3 * l)
        lamp = jnp.zeros((1, 8, DA_QK), F32)
        lamp = lamp.at[0, 0].set(lam_q1[l]).at[0, 1].set(lam_k1[l]).at[0, 2].set(lam_q2[l]).at[0, 3].set(lam_k2[l])
        lamp = lamp.at[0, 4].set(lam_init)
        dnw = jnp.concatenate([diff_norm_w[l], diff_norm_w[l]]).reshape(1, 1, LANES)
        mix_a = _diff_attn(qa, ka, vt, qmask, lamp, dnw)

        mix_b = _na_attn(qn, kn, vn, bias_all[l:l + 1], hmask, C)
        mix_c = _mla_attn(qm, km, vt)

        x1 = _outproj(xs, mix_a, mix_b, mix_c, mod_tok, w_out[l].astype(BF16),
                      ln1_g[l].reshape(1, D), ln1_b[l].reshape(1, D), alpha)

        nck = D_FF // FF_CHUNK
        wu = w_up[l].astype(BF16)
        wg = wu[:, :D_FF].reshape(D, nck, FF_CHUNK).transpose(1, 0, 2)
        wv = wu[:, D_FF:].reshape(D, nck, FF_CHUNK).transpose(1, 0, 2)
        cw = jnp.concatenate([conv_w[l], conv_b[l][None], jnp.zeros((4, 2 * D_FF), F32)], axis=0)
        cg = cw[:, :D_FF].reshape(8, nck, FF_CHUNK).transpose(1, 0, 2)
        cv = cw[:, D_FF:].reshape(8, nck, FF_CHUNK).transpose(1, 0, 2)
        wd = w_down[l].astype(BF16).reshape(nck, FF_CHUNK, D)
        xs = _ffn(x1, mod_tok, wg, wv, cg, cv, wd, ln2_g[l].reshape(1, D), ln2_b[l].reshape(1, D), alpha, C)
    return xs[:, C:]
```

```python
import functools
import math

import numpy as np
import jax
import jax.numpy as jnp
from jax import lax
from jax.experimental import pallas as pl
from jax.experimental.pallas import tpu as pltpu

F32 = jnp.float32
BF16 = jnp.bfloat16

GRID_W = 64
ROPE_DIM = 32
ROPE_BASE = 10000.0
DA_HEADS, DA_QK, DA_V = 6, 32, 64
NA_HEADS, NA_DIM, NA_KH, NA_KW = 6, 64, 8, 16
MLA_HEADS, MLA_Q_RANK, MLA_KV_RANK, MLA_NOPE, MLA_ROPE, MLA_V = 4, 256, 128, 64, 32, 64
D_FF = 2816
LN_EPS = 1e-6
NEG_INF = -1e30
DA_SCALE = DA_QK ** -0.5
NA_SCALE = NA_DIM ** -0.5
MLA_SCALE = (MLA_NOPE + MLA_ROPE) ** -0.5
LOG2E = math.log2(math.e)

T = 256
FLASH_UNROLL = 16
LANES = 128
VT_ROWS = 80
FF_CHUNK = D_FF // 2
NA_WIN_ROWS = 12
HALO = 8
VMEM_LIMIT = 56 * 1024 * 1024

_NT = (((1,), (1,)), ((), ()))


def _layernorm(x):
    mu = jnp.mean(x, axis=-1, keepdims=True)
    xc = x - mu
    var = jnp.mean(xc * xc, axis=-1, keepdims=True)
    return xc * lax.rsqrt(var + LN_EPS)


def _rms(x, g):
    return (x * lax.rsqrt(jnp.mean(x * x, axis=-1, keepdims=True) + LN_EPS)) * g


def _cparams(sem):
    return pltpu.CompilerParams(dimension_semantics=sem, vmem_limit_bytes=VMEM_LIMIT)


def _ada_kernel(cond_ref, w_ref, b_ref, o_ref):
    c = cond_ref[...]
    a = (c * jax.nn.sigmoid(c)).astype(BF16)
    o_ref[0] = jnp.dot(a, w_ref[0].astype(BF16), preferred_element_type=F32) + b_ref[0]


def _ada(cond8, w_ada, b_ada):
    L, D, N = w_ada.shape
    tn = 1536
    return pl.pallas_call(
        _ada_kernel,
        grid=(L, N // tn),
        in_specs=[pl.BlockSpec((8, D), lambda l, j: (0, 0)),
                  pl.BlockSpec((1, D, tn), lambda l, j: (l, 0, j)),
                  pl.BlockSpec((1, 1, tn), lambda l, j: (l, 0, j))],
        out_specs=pl.BlockSpec((1, 8, tn), lambda l, j: (l, 0, j)),
        out_shape=jax.ShapeDtypeStruct((L, 8, N), F32),
        compiler_params=_cparams(("arbitrary", "arbitrary")),
        name="ada",
    )(cond8, w_ada, b_ada.reshape(L, 1, N))


def _inproj_kernel(x_ref, mod_ref, cosa_ref, sina_ref, cosm_ref, sinm_ref, w_ref, wvt_ref,
                   gq_ref, gkv_ref, wuq_ref, wukvk_ref, wukvvt_ref,
                   qa_ref, ka_ref, qn_ref, kn_ref, vn_ref, qm_ref, km_ref, vt_ref):
    D = x_ref.shape[-1]
    x = x_ref[0]
    mod = mod_ref[0, 0]
    h = (_layernorm(x) * (1.0 + mod[:, D:2 * D]) + mod[:, 0:D]).astype(BF16)
    proj = jnp.dot(h, w_ref[...], preferred_element_type=F32)

    cosa, sina = cosa_ref[...], sina_ref[...]
    cos3 = jnp.concatenate([cosa] * 3, axis=1)
    sin3 = jnp.concatenate([sina] * 3, axis=1)
    qa_ref[0] = ((proj[:, 0:384] * cos3 + proj[:, 384:768] * sin3) * (DA_SCALE * LOG2E)).astype(BF16)
    ka_ref[0] = (proj[:, 768:1152] * cos3 + proj[:, 1152:1536] * sin3).astype(BF16)
    qn_ref[0] = (proj[:, 1536:1920] * NA_SCALE).astype(BF16)
    kn_ref[0] = proj[:, 1920:2304].astype(BF16)
    vn_ref[0] = proj[:, 2304:2688].astype(BF16)

    cosm, sinm = cosm_ref[...], sinm_ref[...]
    cos4 = jnp.concatenate([cosm] * 4, axis=1)
    sin4 = jnp.concatenate([sinm] * 4, axis=1)
    cqn = _rms(proj[:, 2688:2944], gq_ref[...]).astype(BF16)
    qc = jnp.dot(cqn, wuq_ref[...], preferred_element_type=F32)
    qm_ref[0] = ((qc[:, 0:512] * cos4 + qc[:, 512:1024] * sin4) * (MLA_SCALE * LOG2E)).astype(BF16)
    kvn = _rms(proj[:, 2944:3072], gkv_ref[...]).astype(BF16)
    kr = proj[:, 3072:3200] * cosm + proj[:, 3200:3328] * sinm
    km = jnp.dot(kvn, wukvk_ref[...], preferred_element_type=F32) + jnp.concatenate([kr] * 4, axis=1)
    km_ref[0] = km.astype(BF16)

    vta = lax.dot_general(wvt_ref[...], h, _NT, preferred_element_type=F32)
    vtm = lax.dot_general(wukvvt_ref[...], kvn, _NT, preferred_element_type=F32)
    ones = jnp.ones((VT_ROWS - DA_V, x.shape[0]), BF16)
    for hd in range(MLA_HEADS):
        vt_ref[0, hd, 0, 0:MLA_V, :] = vtm[hd * MLA_V:(hd + 1) * MLA_V].astype(BF16)
        vt_ref[0, hd, 0, MLA_V:VT_ROWS, :] = ones
    for hd in range(DA_HEADS):
        vt_ref[0, MLA_HEADS + hd, 0, 0:DA_V, :] = vta[hd * DA_V:(hd + 1) * DA_V].astype(BF16)
        vt_ref[0, MLA_HEADS + hd, 0, DA_V:VT_ROWS, :] = ones


def _inproj(xs, mod_tok, tabs, wts):
    B, S, D = xs.shape
    nt = S // T
    cosa, sina, cosm, sinm = tabs
    w_ext, wvt, gq, gkv, wuq, wukvk, wukvvt = wts
    tok = lambda w: pl.BlockSpec((1, T, w), lambda b, i: (b, i, 0))
    tab = pl.BlockSpec((T, LANES), lambda b, i: (i, 0))
    full = lambda a: pl.BlockSpec(a.shape, lambda b, i: (0,) * a.ndim)
    nvt = DA_HEADS + MLA_HEADS
    out_shape = [jax.ShapeDtypeStruct((B, S, w), BF16) for w in (384, 384, 384, 384, 384, 512, 512)]
    out_shape.append(jax.ShapeDtypeStruct((B, nvt, nt, VT_ROWS, T), BF16))
    out_specs = [tok(w) for w in (384, 384, 384, 384, 384, 512, 512)]
    out_specs.append(pl.BlockSpec((1, nvt, 1, VT_ROWS, T), lambda b, i: (b, 0, i, 0, 0)))
    return pl.pallas_call(
        _inproj_kernel,
        grid=(B, nt),
        in_specs=[tok(D),
                  pl.BlockSpec((1, 1, 1, 6 * D), lambda b, i: (b, jnp.minimum(i, 1), 0, 0)),
                  tab, tab, tab, tab,
                  full(w_ext), full(wvt), full(gq), full(gkv), full(wuq), full(wukvk), full(wukvvt)],
        out_specs=out_specs,
        out_shape=out_shape,
        compiler_params=_cparams(("arbitrary", "arbitrary")),
        name="inproj",
    )(xs, mod_tok, cosa, sina, cosm, sinm, w_ext, wvt, gq, gkv, wuq, wukvk, wukvvt)


def _flash_t(qts, k_rows_fn, vt_tile_fn, ctx_only, nchunks, scratch):
    p_sc, m_sc, acc_sc = scratch
    nsub = len(qts)

    def cols(s):
        return slice(s * T, (s + 1) * T)

    def scores(j):
        start = pl.multiple_of(j * T, T)
        return [jnp.dot(k_rows_fn(start, T, s), qts[s], preferred_element_type=F32) for s in range(nsub)]

    st0 = scores(0)
    for s in range(nsub):
        m0 = jnp.max(st0[s], axis=0, keepdims=True)
        m_sc[:, cols(s)] = m0
        p_sc[:, cols(s)] = jnp.exp2(st0[s] - m0).astype(BF16)
    acc_sc[...] = jnp.zeros(acc_sc.shape, F32)

    def body(i, carry):
        pvs, p_prev = [None] * nsub, None
        for c in range(FLASH_UNROLL):
            j = 1 + i * FLASH_UNROLL + c
            st = scores(j)
            for s in range(nsub):
                p = p_sc[:, cols(s)] if c == 0 else p_prev[s]
                d = jnp.dot(vt_tile_fn(j - 1, s), p, preferred_element_type=F32)
                pvs[s] = d if c == 0 else pvs[s] + d
            p_prev = [jnp.exp2(st[s] - m_sc[:, cols(s)]).astype(BF16) for s in range(nsub)]
        for s in range(nsub):
            p_sc[:, cols(s)] = p_prev[s]
            acc_sc[:, cols(s)] += pvs[s]
        return carry

    ntrips = jnp.where(ctx_only, 0, (nchunks - 1) // FLASH_UNROLL)
    lax.fori_loop(0, ntrips, body, 0)
    last = jnp.where(ctx_only, 0, nchunks - 1)
    for s in range(nsub):
        acc_sc[:, cols(s)] += jnp.dot(vt_tile_fn(last, s), p_sc[:, cols(s)], preferred_element_type=F32)

    overflowed = jnp.max(jnp.where(jnp.isfinite(acc_sc[...]), 0.0, 1.0)) > 0.5

    @pl.when(overflowed)
    def _():
        m_sc[...] = jnp.full(m_sc.shape, NEG_INF, F32)
        acc_sc[...] = jnp.zeros(acc_sc.shape, F32)

        def exact(j, carry):
            st = scores(j)
            for s in range(nsub):
                m_old = m_sc[:, cols(s)]
                m_new = jnp.maximum(m_old, jnp.max(st[s], axis=0, keepdims=True))
                p = jnp.exp2(st[s] - m_new).astype(BF16)
                pv = jnp.dot(vt_tile_fn(j, s), p, preferred_element_type=F32)
                acc_sc[:, cols(s)] = jnp.exp2(m_old - m_new) * acc_sc[:, cols(s)] + pv
                m_sc[:, cols(s)] = m_new
            return carry

        lax.fori_loop(0, last + 1, exact, 0)

    acc = acc_sc[...]
    return [acc[0:64, cols(s)] / acc[64:65, cols(s)] for s in range(nsub)]


def _flash_scratch(nsub):
    w = nsub * T
    return [pltpu.VMEM((T, w), BF16),
            pltpu.VMEM((1, w), F32), pltpu.VMEM((VT_ROWS, w), F32)]


def _diff_kernel(q_ref, k_ref, vt_ref, qmask_ref, lamp_ref, dnw_ref, o_ref, *scratch):
    ctx_only = pl.program_id(2) == 0
    nchunks = k_ref.shape[1] // T
    qf = q_ref[0].astype(F32)
    qts = [(qf * qmask_ref[s:s + 1, :]).T.astype(BF16) for s in range(4)]

    def k_rows(start, n, s):
        return k_ref[0, pl.ds(start, n), :]

    def vt_tile(j, s):
        return vt_ref[0, s // 2, j]

    outs = _flash_t(qts, k_rows, vt_tile, ctx_only, nchunks, scratch)
    lp = lamp_ref[0]
    lam_init = lp[4:5, 0:1]
    lam = (jnp.exp(jnp.sum(lp[0:1] * lp[1:2], axis=1, keepdims=True))
           - jnp.exp(jnp.sum(lp[2:3] * lp[3:4], axis=1, keepdims=True)) + lam_init)
    ys = []
    for hd in range(2):
        o = outs[2 * hd] - lam * outs[2 * hd + 1]
        ys.append(o * lax.rsqrt(jnp.mean(o * o, axis=0, keepdims=True) + LN_EPS))
    y = jnp.concatenate(ys, axis=0).T
    o_ref[0] = ((y * dnw_ref[0]) * (1.0 - lam_init)).astype(o_ref.dtype)


def _diff_attn(qa, ka, vt, qmask, lamp_l, dnw_l):
    B, S, _ = qa.shape
    nt = S // T
    G = DA_HEADS // 2
    return pl.pallas_call(
        _diff_kernel,
        grid=(B, G, nt),
        in_specs=[pl.BlockSpec((1, T, LANES), lambda b, g, i: (b, i, g)),
                  pl.BlockSpec((1, S, LANES), lambda b, g, i: (b, 0, g)),
                  pl.BlockSpec((1, 2, nt, VT_ROWS, T), lambda b, g, i: (b, MLA_HEADS // 2 + g, 0, 0, 0)),
                  pl.BlockSpec((8, LANES), lambda b, g, i: (0, 0)),
                  pl.BlockSpec((1, 8, DA_QK), lambda b, g, i: (0, 0, 0)),
                  pl.BlockSpec((1, 1, LANES), lambda b, g, i: (0, 0, 0))],
        out_specs=pl.BlockSpec((1, T, LANES), lambda b, g, i: (b, i, g)),
        out_shape=jax.ShapeDtypeStruct((B, S, DA_HEADS * DA_V), BF16),
        scratch_shapes=_flash_scratch(4),
        compiler_params=_cparams(("arbitrary", "arbitrary", "arbitrary")),
        name="diff_attn",
    )(qa, ka, vt, qmask, lamp_l, dnw_l)


def _mla_kernel(q_ref, k_ref, vt_ref, o_ref, *scratch):
    ctx_only = pl.program_id(1) == 0
    nchunks = k_ref.shape[1] // T
    qf = q_ref[0].astype(F32)
    qts = [qf[:, s * LANES:(s + 1) * LANES].T.astype(BF16) for s in range(MLA_HEADS)]

    def k_rows(start, n, s):
        return k_ref[0, pl.ds(start, n), s * LANES:(s + 1) * LANES]

    def vt_tile(j, s):
        return vt_ref[0, s, j]

    outs = _flash_t(qts, k_rows, vt_tile, ctx_only, nchunks, scratch)
    o_ref[0] = jnp.concatenate(outs, axis=0).T.astype(o_ref.dtype)


def _mla_attn(qm, km, vt):
    B, S, W = qm.shape
    nt = S // T
    return pl.pallas_call(
        _mla_kernel,
        grid=(B, nt),
        in_specs=[pl.BlockSpec((1, T, W), lambda b, i: (b, i, 0)),
                  pl.BlockSpec((1, S, W), lambda b, i: (b, 0, 0)),
                  pl.BlockSpec((1, MLA_HEADS, nt, VT_ROWS, T), lambda b, i: (b, 0, 0, 0, 0))],
        out_specs=pl.BlockSpec((1, T, MLA_HEADS * MLA_V), lambda b, i: (b, i, 0)),
        out_shape=jax.ShapeDtypeStruct((B, S, MLA_HEADS * MLA_V), BF16),
        scratch_shapes=_flash_scratch(MLA_HEADS),
        compiler_params=_cparams(("arbitrary", "arbitrary")),
        name="mla_attn",
    )(qm, km, vt)


def _na_kernel(q_ref, k_ref, v_ref, bias_ref, hmask_ref, o_ref, *, ctx_len, rows):
    i = pl.program_id(2)
    qf = q_ref[0].astype(F32)
    lane_head = hmask_ref[...]
    k_ctx = k_ref[0, 0:ctx_len, :]
    v_ctx = v_ref[0, 0:ctx_len, :]

    @pl.when(i == 0)
    def _():
        out = jnp.zeros((T, LANES), F32)
        for hd in range(2):
            hm = lane_head[hd:hd + 1, :]
            qh = (qf * hm).astype(BF16)
            s = lax.dot_general(qh, k_ctx, _NT, preferred_element_type=F32)
            m = jnp.max(s, axis=1, keepdims=True)
            p = jnp.exp(s - m)
            l = jnp.sum(p, axis=1, keepdims=True)
            o = jnp.dot(p.astype(BF16), v_ctx, preferred_element_type=F32) / l
            out = out + o * hm
        o_ref[0] = out.astype(o_ref.dtype)

    @pl.when(i > 0)
    def _():
        r = (i - 1) * (T // GRID_W)
        start = jnp.clip(r - NA_KH // 2, 0, rows - NA_WIN_ROWS)
        off = pl.multiple_of(ctx_len + start * GRID_W, GRID_W)
        k_win = k_ref[0, pl.ds(off, NA_WIN_ROWS * GRID_W), :]
        v_win = v_ref[0, pl.ds(off, NA_WIN_ROWS * GRID_W), :]
        out = jnp.zeros((T, LANES), F32)
        for hd in range(2):
            hm = lane_head[hd:hd + 1, :]
            qh = (qf * hm).astype(BF16)
            s_w = lax.dot_general(qh, k_win, _NT, preferred_element_type=F32) + bias_ref[0, hd, 0]
            s_c = lax.dot_general(qh, k_ctx, _NT, preferred_element_type=F32)
            m = jnp.maximum(jnp.max(s_w, axis=1, keepdims=True), jnp.max(s_c, axis=1, keepdims=True))
            p_w = jnp.exp(s_w - m)
            p_c = jnp.exp(s_c - m)
            l = jnp.sum(p_w, axis=1, keepdims=True) + jnp.sum(p_c, axis=1, keepdims=True)
            o = (jnp.dot(p_w.astype(BF16), v_win, preferred_element_type=F32)
                 + jnp.dot(p_c.astype(BF16), v_ctx, preferred_element_type=F32)) / l
            out = out + o * hm
        o_ref[0] = out.astype(o_ref.dtype)


def _na_attn(qn, kn, vn, bias_l, hmask, ctx_len):
    B, S, _ = qn.shape
    nt = S // T
    rows = (S - ctx_len) // GRID_W
    G = NA_HEADS // 2
    nlat = nt - ctx_len // T

    def bias_map(b, g, i):
        variant = jnp.where(i <= 1, 0, jnp.where(i == nlat, 2, 1))
        return (0, g, variant, 0, 0)

    return pl.pallas_call(
        functools.partial(_na_kernel, ctx_len=ctx_len, rows=rows),
        grid=(B, G, nt),
        in_specs=[pl.BlockSpec((1, T, LANES), lambda b, g, i: (b, i, g)),
                  pl.BlockSpec((1, S, LANES), lambda b, g, i: (b, 0, g)),
                  pl.BlockSpec((1, S, LANES), lambda b, g, i: (b, 0, g)),
                  pl.BlockSpec((1, 2, 1, T, NA_WIN_ROWS * GRID_W), bias_map),
                  pl.BlockSpec((8, LANES), lambda b, g, i: (0, 0))],
        out_specs=pl.BlockSpec((1, T, LANES), lambda b, g, i: (b, i, g)),
        out_shape=jax.ShapeDtypeStruct((B, S, NA_HEADS * NA_DIM), BF16),
        compiler_params=_cparams(("arbitrary", "arbitrary", "arbitrary")),
        name="na_attn",
    )(qn, kn, vn, bias_l, hmask)


def _outproj_kernel(x_ref, ma_ref, mb_ref, mc_ref, mod_ref, w_ref, g_ref, b_ref, o_ref, *, alpha):
    D = x_ref.shape[-1]
    mix = jnp.concatenate([ma_ref[0], mb_ref[0], mc_ref[0]], axis=1)
    y = jnp.dot(mix, w_ref[...], preferred_element_type=F32)
    gate = mod_ref[0, 0][:, 2 * D:3 * D]
    o_ref[0] = _layernorm(alpha * x_ref[0] + gate * y) * g_ref[...] + b_ref[...]


def _outproj(xs, mix_a, mix_b, mix_c, mod_tok, w_out, g1, b1, alpha):
    B, S, D = xs.shape
    nt = S // T
    tok = lambda w: pl.BlockSpec((1, T, w), lambda b, i: (b, i, 0))
    full = lambda a: pl.BlockSpec(a.shape, lambda b, i: (0,) * a.ndim)
    return pl.pallas_call(
        functools.partial(_outproj_kernel, alpha=alpha),
        grid=(B, nt),
        in_specs=[tok(D), tok(mix_a.shape[-1]), tok(mix_b.shape[-1]), tok(mix_c.shape[-1]),
                  pl.BlockSpec((1, 1, 1, 6 * D), lambda b, i: (b, jnp.minimum(i, 1), 0, 0)),
                  full(w_out), full(g1), full(b1)],
        out_specs=tok(D),
        out_shape=jax.ShapeDtypeStruct((B, S, D), F32),
        compiler_params=_cparams(("arbitrary", "arbitrary")),
        name="outproj",
    )(xs, mix_a, mix_b, mix_c, mod_tok, w_out, g1, b1)


def _ffn_kernel(x_ref, xp_ref, xn_ref, mod_ref, wg_ref, wv_ref, cg_ref, cv_ref, wd_ref, g_ref, b_ref,
                o_ref, *u_refs, alpha, nlat_first, nlast):
    D = x_ref.shape[-1]
    i = pl.program_id(1)
    mod = mod_ref[0, 0]
    shift, scale, gate = mod[:, 3 * D:4 * D], mod[:, 4 * D:5 * D], mod[:, 5 * D:6 * D]
    x = x_ref[0]
    has_prev = jnp.logical_and(i != 0, i != nlat_first)
    has_next = jnp.logical_and(i != nlat_first - 1, i != nlast)
    hp = (_layernorm(xp_ref[0]) * (1.0 + scale) + shift) * jnp.where(has_prev, 1.0, 0.0)
    hn = (_layernorm(xn_ref[0]) * (1.0 + scale) + shift) * jnp.where(has_next, 1.0, 0.0)
    hc = _layernorm(x) * (1.0 + scale) + shift
    h = jnp.concatenate([hp, hc, hn], axis=0).astype(BF16)
    n = x.shape[0]
    nck = wg_ref.shape[0]

    for j in range(nck):
        u_refs[2 * j][...] = jnp.dot(h, wg_ref[j], preferred_element_type=F32)
        u_refs[2 * j + 1][...] = jnp.dot(h, wv_ref[j], preferred_element_type=F32)

    def conv(u_ref, c):
        return (u_ref[HALO - 1:HALO - 1 + n, :] * c[0:1] + u_ref[HALO:HALO + n, :] * c[1:2]
                + u_ref[HALO + 1:HALO + 1 + n, :] * c[2:3] + c[3:4])

    y = None
    for j in range(nck):
        ug = conv(u_refs[2 * j], cg_ref[j])
        uv = conv(u_refs[2 * j + 1], cv_ref[j])
        a = ((ug * jax.nn.sigmoid(ug)) * uv).astype(BF16)
        d = jnp.dot(a, wd_ref[j], preferred_element_type=F32)
        y = d if y is None else y + d
    o_ref[0] = _layernorm(alpha * x + gate * y) * g_ref[...] + b_ref[...]


def _ffn(xs, mod_tok, wg, wv, cg, cv, wd, g2, b2, alpha, ctx_len):
    B, S, D = xs.shape
    nt = S // T
    th = T // HALO
    nh = S // HALO
    tok = pl.BlockSpec((1, T, D), lambda b, i: (b, i, 0))
    full = lambda a: pl.BlockSpec(a.shape, lambda b, i: (0,) * a.ndim)
    return pl.pallas_call(
        functools.partial(_ffn_kernel, alpha=alpha, nlat_first=ctx_len // T, nlast=nt - 1),
        grid=(B, nt),
        in_specs=[tok,
                  pl.BlockSpec((1, HALO, D), lambda b, i: (b, jnp.maximum(i * th - 1, 0), 0)),
                  pl.BlockSpec((1, HALO, D), lambda b, i: (b, jnp.minimum((i + 1) * th, nh - 1), 0)),
                  pl.BlockSpec((1, 1, 1, 6 * D), lambda b, i: (b, jnp.minimum(i, 1), 0, 0)),
                  full(wg), full(wv), full(cg), full(cv), full(wd), full(g2), full(b2)],
        out_specs=tok,
        out_shape=jax.ShapeDtypeStruct((B, S, D), F32),
        scratch_shapes=[pltpu.VMEM((T + 2 * HALO, wg.shape[-1]), F32) for _ in range(2 * wg.shape[0])],
        compiler_params=_cparams(("arbitrary", "arbitrary")),
        name="ffn",
    )(xs, xs, xs, mod_tok, wg, wv, cg, cv, wd, g2, b2)


def _rot_perm(width):
    j = np.arange(width)
    jj = j % (ROPE_DIM // 2)
    lo = jj < ROPE_DIM // 4
    src = np.where(lo, j + ROPE_DIM // 4, j - ROPE_DIM // 4)
    sign = np.where(lo, -1.0, 1.0).astype(np.float32)
    return src, sign


def _rope_tables(n, ctx_len):
    t = jnp.arange(n)
    row = (t // GRID_W).astype(F32)
    col = (t % GRID_W).astype(F32)
    axis_dim = ROPE_DIM // 2
    inv_freq = ROPE_BASE ** (-jnp.arange(0, axis_dim, 2, dtype=F32) / axis_dim)

    def cs(pos):
        ang = pos[:, None] * inv_freq[None, :]
        ang = jnp.concatenate([ang, ang], axis=-1)
        return jnp.cos(ang), jnp.sin(ang)

    cr, sr = cs(row)
    cc, sc = cs(col)
    cos32 = jnp.concatenate([cr, cc], axis=-1)
    sin32 = jnp.concatenate([sr, sc], axis=-1)
    cos32 = jnp.concatenate([jnp.ones((ctx_len, ROPE_DIM), F32), cos32], axis=0)
    sin32 = jnp.concatenate([jnp.zeros((ctx_len, ROPE_DIM), F32), sin32], axis=0)
    S = n + ctx_len
    cosa = jnp.tile(cos32, (1, LANES // ROPE_DIM))
    sina = jnp.tile(sin32, (1, LANES // ROPE_DIM))
    cosm = jnp.concatenate([jnp.ones((S, MLA_NOPE), F32), cos32, jnp.ones((S, LANES - MLA_NOPE - MLA_ROPE), F32)], axis=1)
    sinm = jnp.concatenate([jnp.zeros((S, MLA_NOPE), F32), sin32, jnp.zeros((S, LANES - MLA_NOPE - MLA_ROPE), F32)], axis=1)
    return cosa, sina, cosm, sinm


def _na_block_index(rows):
    qrows = T // GRID_W
    masked = 2 * NA_KH - 1
    blk = np.full((3, qrows, NA_WIN_ROWS), masked, np.int32)
    for v, r in enumerate((0, 2 * qrows, rows - qrows)):
        start = int(np.clip(r - NA_KH // 2, 0, rows - NA_WIN_ROWS))
        for a in range(qrows):
            qr = r + a
            r0 = int(np.clip(qr - NA_KH // 2, 0, rows - NA_KH))
            for kb in range(NA_WIN_ROWS):
                kr = start + kb
                if r0 <= kr < r0 + NA_KH:
                    blk[v, a, kb] = kr - qr + (NA_KH - 1)
    return blk


def _bias_block_kernel(rpb_ref, dc_ref, band_ref, o_ref):
    rpb = rpb_ref[0]
    dc = dc_ref[...]
    out = jnp.full(o_ref.shape[1:], NEG_INF, F32)
    for d in range(rpb.shape[1]):
        out = jnp.where(dc == d, rpb[:, d:d + 1], out)
    o_ref[0] = jnp.where(band_ref[...] > 0, out, NEG_INF)


def _na_bias_tiles(na_rpb, rows):
    L, H, ndr, ndc = na_rpb.shape
    qc = np.arange(GRID_W)
    c0 = np.clip(qc - NA_KW // 2, 0, GRID_W - NA_KW)
    band = (qc[None, :] >= c0[:, None]) & (qc[None, :] < c0[:, None] + NA_KW)
    dcm = np.clip(qc[None, :] - qc[:, None], -(NA_KW - 1), NA_KW - 1) + (NA_KW - 1)
    ww = GRID_W * GRID_W
    tb = pl.pallas_call(
        _bias_block_kernel,
        grid=(L,),
        in_specs=[pl.BlockSpec((1, H * ndr, ndc), lambda l: (l, 0, 0)),
                  pl.BlockSpec((1, ww), lambda l: (0, 0)),
                  pl.BlockSpec((1, ww), lambda l: (0, 0))],
        out_specs=pl.BlockSpec((1, H * ndr, ww), lambda l: (l, 0, 0)),
        out_shape=jax.ShapeDtypeStruct((L, H * ndr, ww), F32),
        compiler_params=_cparams(("arbitrary",)),
        name="na_bias_blocks",
    )(na_rpb.reshape(L, H * ndr, ndc), jnp.asarray(dcm.reshape(1, ww), jnp.int32),
      jnp.asarray(band.reshape(1, ww), jnp.int32))
    tb = tb.reshape(L, H, ndr, GRID_W, GRID_W)
    tb = jnp.concatenate([tb, jnp.full((L, H, 1, GRID_W, GRID_W), NEG_INF, F32)], axis=2)
    blk = _na_block_index(rows)
    tiles = tb[:, :, blk]
    return tiles.transpose(0, 1, 2, 3, 5, 4, 6).reshape(L, H, 3, T, NA_WIN_ROWS * GRID_W)


def kernel(x, c, ctx, c_ctx, w_ada, b_ada, w_in, lam_q1, lam_k1, lam_q2, lam_k2, diff_norm_w, na_rpb,
           mla_q_norm_w, mla_kv_norm_w, w_uq, w_ukv, w_out, ln1_g, ln1_b, w_up, conv_w, conv_b, w_down,
           ln2_g, ln2_b):
    B, N, D = x.shape
    C = ctx.shape[1]
    L = w_in.shape[0]
    S = C + N
    rows = N // GRID_W
    assert C == T and N % (T * FLASH_UNROLL) == 0
    assert rows >= NA_WIN_ROWS and B + 1 <= 8 and D_FF % FF_CHUNK == 0
    alpha = (2 * L) ** 0.25

    cond8 = jnp.zeros((8, D), F32).at[0:B].set(c).at[B].set(c_ctx)
    mod_all = _ada(cond8, w_ada, b_ada)
    ctx_rows = jnp.broadcast_to(mod_all[:, B][:, None], (L, B, 6 * D))
    mod_tok_all = jnp.stack([ctx_rows, mod_all[:, 0:B]], axis=2).reshape(L, B, 2, 1, 6 * D)

    tabs = _rope_tables(N, C)
    lane = np.arange(LANES)
    qmask = np.zeros((8, LANES), np.float32)
    for s in range(4):
        qmask[s] = (lane // DA_QK == s)
    hmask = np.zeros((8, LANES), np.float32)
    for hd in range(2):
        hmask[hd] = (lane // NA_DIM == hd)
    qmask, hmask = jnp.asarray(qmask), jnp.asarray(hmask)
    bias_all = _na_bias_tiles(na_rpb, rows)

    src_a, sgn_a = _rot_perm(DA_HEADS * 2 * DA_QK)
    src_r, sgn_r = _rot_perm(MLA_ROPE)
    sp = np.cumsum([0, 384, 384, 384, 384, 384, 384, MLA_Q_RANK, MLA_KV_RANK, MLA_ROPE])
    zeros = lambda *s: jnp.zeros(s, F32)

    xs = jnp.concatenate([ctx, x], axis=1)
    for l in range(L):
        wi = w_in[l]
        a_q, a_k, a_v, n_q, n_k, n_v, c_q, c_kv, k_r = [wi[:, sp[t]:sp[t + 1]] for t in range(9)]
        kr_pad = jnp.concatenate([zeros(D, MLA_NOPE), k_r, zeros(D, LANES - MLA_NOPE - MLA_ROPE)], axis=1)
        krr_pad = jnp.concatenate([zeros(D, MLA_NOPE), k_r[:, src_r] * sgn_r,
                                   zeros(D, LANES - MLA_NOPE - MLA_ROPE)], axis=1)
        w_ext = jnp.concatenate([a_q, a_q[:, src_a] * sgn_a, a_k, a_k[:, src_a] * sgn_a, n_q, n_k, n_v,
                                 c_q, c_kv, kr_pad, krr_pad], axis=1).astype(BF16)
        wvt = a_v.T.astype(BF16)
        uq = w_uq[l].reshape(MLA_Q_RANK, MLA_HEADS, MLA_NOPE + MLA_ROPE)
        uq_rope = uq[:, :, MLA_NOPE:]
        padq = jnp.zeros((MLA_Q_RANK, MLA_HEADS, LANES - MLA_NOPE - MLA_ROPE), F32)
        wuq = jnp.concatenate([
            jnp.concatenate([uq, padq], axis=2).reshape(MLA_Q_RANK, MLA_HEADS * LANES),
            jnp.concatenate([jnp.zeros_like(uq[:, :, :MLA_NOPE]), uq_rope[:, :, src_r] * sgn_r, padq],
                            axis=2).reshape(MLA_Q_RANK, MLA_HEADS * LANES)], axis=1).astype(BF16)
        ukv = w_ukv[l].reshape(MLA_KV_RANK, MLA_HEADS, MLA_NOPE + MLA_V)
        wukvk = jnp.concatenate([ukv[:, :, :MLA_NOPE], jnp.zeros((MLA_KV_RANK, MLA_HEADS, LANES - MLA_NOPE), F32)],
                                axis=2).reshape(MLA_KV_RANK, MLA_HEADS * LANES).astype(BF16)
        wukvvt = ukv[:, :, MLA_NOPE:].reshape(MLA_KV_RANK, MLA_HEADS * MLA_V).T.astype(BF16)
        gq = mla_q_norm_w[l].reshape(1, MLA_Q_RANK)
        gkv = mla_kv_norm_w[l].reshape(1, MLA_KV_RANK)
        mod_tok = mod_tok_all[l]

        qa, ka, qn, kn, vn, qm, km, vt = _inproj(xs, mod_tok, tabs, (w_ext, wvt, gq, gkv, wuq, wukvk, wukvvt))

        lam_init = 0.8 - 0.6 * math.exp(-0.3 * l)
        lamp = jnp.zeros((1, 8, DA_QK), F32)
        lamp = lamp.at[0, 0].set(lam_q1[l]).at[0, 1].set(lam_k1[l]).at[0, 2].set(lam_q2[l]).at[0, 3].set(lam_k2[l])
        lamp = lamp.at[0, 4].set(lam_init)
        dnw = jnp.concatenate([diff_norm_w[l], diff_norm_w[l]]).reshape(1, 1, LANES)
        mix_a = _diff_attn(qa, ka, vt, qmask, lamp, dnw)

        mix_b = _na_attn(qn, kn, vn, bias_all[l:l + 1], hmask, C)
        mix_c = _mla_attn(qm, km, vt)

        x1 = _outproj(xs, mix_a, mix_b, mix_c, mod_tok, w_out[l].astype(BF16),
                      ln1_g[l].reshape(1, D), ln1_b[l].reshape(1, D), alpha)

        nck = D_FF // FF_CHUNK
        wu = w_up[l].astype(BF16)
        wg = wu[:, :D_FF].reshape(D, nck, FF_CHUNK).transpose(1, 0, 2)
        wv = wu[:, D_FF:].reshape(D, nck, FF_CHUNK).transpose(1, 0, 2)
        cw = jnp.concatenate([conv_w[l], conv_b[l][None], jnp.zeros((4, 2 * D_FF), F32)], axis=0)
        cg = cw[:, :D_FF].reshape(8, nck, FF_CHUNK).transpose(1, 0, 2)
        cv = cw[:, D_FF:].reshape(8, nck, FF_CHUNK).transpose(1, 0, 2)
        wd = w_down[l].astype(BF16).reshape(nck, FF_CHUNK, D)
        xs = _ffn(x1, mod_tok, wg, wv, cg, cv, wd, ln2_g[l].reshape(1, D), ln2_b[l].reshape(1, D), alpha, C)
    return xs[:, C:]
```

```python
import functools
import math

import numpy as np
import jax
import jax.numpy as jnp
from jax import lax
from jax.experimental import pallas as pl
from jax.experimental.pallas import tpu as pltpu

F32 = jnp.float32
BF16 = jnp.bfloat16

GRID_W = 64
ROPE_DIM = 32
ROPE_BASE = 10000.0
DA_HEADS, DA_QK, DA_V = 6, 32, 64
NA_HEADS, NA_DIM, NA_KH, NA_KW = 6, 64, 8, 16
MLA_HEADS, MLA_Q_RANK, MLA_KV_RANK, MLA_NOPE, MLA_ROPE, MLA_V = 4, 256, 128, 64, 32, 64
D_FF = 2816
LN_EPS = 1e-6
NEG_INF = -1e30
DA_SCALE = DA_QK ** -0.5
NA_SCALE = NA_DIM ** -0.5
MLA_SCALE = (MLA_NOPE + MLA_ROPE) ** -0.5
LOG2E = math.log2(math.e)

T = 256
FLASH_UNROLL = 32
MXU_ORDER = (("qk", 0), ("pv", 0), ("qk", 1), ("pv", 1), ("qk", 2), ("pv", 2), ("qk", 3), ("pv", 3))
LANES = 128
VT_ROWS = 64
FF_SPLITS = (1024, 1024, 768)
NA_WIN_ROWS = 12
NA_BLOCKS = 2 * NA_KH
HALO = 8
VMEM_LIMIT = 56 * 1024 * 1024

_NT = (((1,), (1,)), ((), ()))


def _layernorm(x):
    mu = jnp.mean(x, axis=-1, keepdims=True)
    xc = x - mu
    var = jnp.mean(xc * xc, axis=-1, keepdims=True)
    return xc * lax.rsqrt(var + LN_EPS)


def _rms(x, g):
    return (x * lax.rsqrt(jnp.mean(x * x, axis=-1, keepdims=True) + LN_EPS)) * g


def _cparams(sem):
    return pltpu.CompilerParams(dimension_semantics=sem, vmem_limit_bytes=VMEM_LIMIT)


def _ada_kernel(cond_ref, w_ref, b_ref, o_ref):
    c = cond_ref[...]
    a = (c * jax.nn.sigmoid(c)).astype(BF16)
    o_ref[0] = jnp.dot(a, w_ref[0].astype(BF16), preferred_element_type=F32) + b_ref[0]


def _ada(cond8, w_ada, b_ada):
    L, D, N = w_ada.shape
    tn = 1536
    return pl.pallas_call(
        _ada_kernel,
        grid=(L, N // tn),
        in_specs=[pl.BlockSpec((8, D), lambda l, j: (0, 0)),
                  pl.BlockSpec((1, D, tn), lambda l, j: (l, 0, j)),
                  pl.BlockSpec((1, 1, tn), lambda l, j: (l, 0, j))],
        out_specs=pl.BlockSpec((1, 8, tn), lambda l, j: (l, 0, j)),
        out_shape=jax.ShapeDtypeStruct((L, 8, N), F32),
        compiler_params=_cparams(("arbitrary", "arbitrary")),
        name="ada",
    )(cond8, w_ada, b_ada.reshape(L, 1, N))


def _inproj_kernel(x_ref, mod_ref, cosa_ref, sina_ref, cosm_ref, sinm_ref, w_ref, wvt_ref,
                   gq_ref, gkv_ref, wuq_ref, wukvk_ref, wukvvt_ref,
                   qa_ref, ka_ref, qn_ref, kn_ref, vn_ref, qm_ref, km_ref, vt_ref):
    D = x_ref.shape[-1]
    x = x_ref[0]
    mod = mod_ref[0, 0]
    h = (_layernorm(x) * (1.0 + mod[:, D:2 * D]) + mod[:, 0:D]).astype(BF16)
    proj = jnp.dot(h, w_ref[...], preferred_element_type=F32)

    cosa, sina = cosa_ref[...], sina_ref[...]
    cos3 = jnp.concatenate([cosa] * 3, axis=1)
    sin3 = jnp.concatenate([sina] * 3, axis=1)
    qa_ref[0] = ((proj[:, 0:384] * cos3 + proj[:, 384:768] * sin3) * (DA_SCALE * LOG2E)).astype(BF16)
    ka_ref[0] = (proj[:, 768:1152] * cos3 + proj[:, 1152:1536] * sin3).astype(BF16)
    qn_ref[0] = (proj[:, 1536:1920] * NA_SCALE).astype(BF16)
    kn_ref[0] = proj[:, 1920:2304].astype(BF16)
    vn_ref[0] = proj[:, 2304:2688].astype(BF16)

    cosm, sinm = cosm_ref[...], sinm_ref[...]
    cos4 = jnp.concatenate([cosm] * 4, axis=1)
    sin4 = jnp.concatenate([sinm] * 4, axis=1)
    cqn = _rms(proj[:, 2688:2944], gq_ref[...]).astype(BF16)
    qc = jnp.dot(cqn, wuq_ref[...], preferred_element_type=F32)
    qm_ref[0] = ((qc[:, 0:512] * cos4 + qc[:, 512:1024] * sin4) * (MLA_SCALE * LOG2E)).astype(BF16)
    kvn = _rms(proj[:, 2944:3072], gkv_ref[...]).astype(BF16)
    kr = proj[:, 3072:3200] * cosm + proj[:, 3200:3328] * sinm
    km = jnp.dot(kvn, wukvk_ref[...], preferred_element_type=F32) + jnp.concatenate([kr] * 4, axis=1)
    km_ref[0] = km.astype(BF16)

    vta = lax.dot_general(wvt_ref[...], h, _NT, preferred_element_type=F32)
    vtm = lax.dot_general(wukvvt_ref[...], kvn, _NT, preferred_element_type=F32)
    for hd in range(MLA_HEADS):
        vt_ref[0, hd, 0] = vtm[hd * MLA_V:(hd + 1) * MLA_V].astype(BF16)
    for hd in range(DA_HEADS):
        vt_ref[0, MLA_HEADS + hd, 0] = vta[hd * DA_V:(hd + 1) * DA_V].astype(BF16)


def _inproj(xs, mod_tok, tabs, wts):
    B, S, D = xs.shape
    nt = S // T
    cosa, sina, cosm, sinm = tabs
    w_ext, wvt, gq, gkv, wuq, wukvk, wukvvt = wts
    tok = lambda w: pl.BlockSpec((1, T, w), lambda b, i: (b, i, 0))
    tab = pl.BlockSpec((T, LANES), lambda b, i: (i, 0))
    full = lambda a: pl.BlockSpec(a.shape, lambda b, i: (0,) * a.ndim)
    nvt = DA_HEADS + MLA_HEADS
    out_shape = [jax.ShapeDtypeStruct((B, S, w), BF16) for w in (384, 384, 384, 384, 384, 512, 512)]
    out_shape.append(jax.ShapeDtypeStruct((B, nvt, nt, VT_ROWS, T), BF16))
    out_specs = [tok(w) for w in (384, 384, 384, 384, 384, 512, 512)]
    out_specs.append(pl.BlockSpec((1, nvt, 1, VT_ROWS, T), lambda b, i: (b, 0, i, 0, 0)))
    return pl.pallas_call(
        _inproj_kernel,
        grid=(B, nt),
        in_specs=[tok(D),
                  pl.BlockSpec((1, 1, 1, 6 * D), lambda b, i: (b, jnp.minimum(i, 1), 0, 0)),
                  tab, tab, tab, tab,
                  full(w_ext), full(wvt), full(gq), full(gkv), full(wuq), full(wukvk), full(wukvvt)],
        out_specs=out_specs,
        out_shape=out_shape,
        compiler_params=_cparams(("arbitrary", "arbitrary")),
        name="inproj",
    )(xs, mod_tok, cosa, sina, cosm, sinm, w_ext, wvt, gq, gkv, wuq, wukvk, wukvvt)


def _flash_t(qts, k_rows_fn, vt_tile_fn, ctx_only, nchunks, scratch):
    p_sc, m_sc, l_sc, acc_sc = scratch
    nsub = len(qts)
    unroll = math.gcd(nchunks - 1, FLASH_UNROLL)

    def cols(s):
        return slice(s * T, (s + 1) * T)

    def scores(j):
        start = pl.multiple_of(j * T, T)
        return [jnp.dot(k_rows_fn(start, T, s), qts[s], preferred_element_type=F32) for s in range(nsub)]

    def sublane_sums(p):
        return jnp.sum(p.reshape(T // 8, 8, p.shape[1]), axis=0)

    st0 = scores(0)
    for s in range(nsub):
        m0 = jnp.max(st0[s], axis=0, keepdims=True)
        p0 = jnp.exp2(st0[s] - m0)
        m_sc[:, cols(s)] = m0
        l_sc[:, cols(s)] = sublane_sums(p0)
        p_sc[:, cols(s)] = p0.astype(BF16)
    acc_sc[...] = jnp.zeros(acc_sc.shape, F32)

    def body(i, carry):
        pvs, ls, p_prev = [None] * nsub, [None] * nsub, None
        for c in range(unroll):
            j = 1 + i * unroll + c
            start = pl.multiple_of(j * T, T)
            st = [None] * nsub
            for kind, s in MXU_ORDER:
                if kind == "qk":
                    st[s] = jnp.dot(k_rows_fn(start, T, s), qts[s], preferred_element_type=F32)
                else:
                    p = p_sc[:, cols(s)] if c == 0 else p_prev[s]
                    d = jnp.dot(vt_tile_fn(j - 1, s), p, preferred_element_type=F32)
                    pvs[s] = d if c == 0 else pvs[s] + d
            p_prev = []
            for s in range(nsub):
                p = jnp.exp2(st[s] - m_sc[:, cols(s)])
                ls[s] = sublane_sums(p) if c == 0 else ls[s] + sublane_sums(p)
                p_prev.append(p.astype(BF16))
        for s in range(nsub):
            p_sc[:, cols(s)] = p_prev[s]
            l_sc[:, cols(s)] += ls[s]
            acc_sc[:, cols(s)] += pvs[s]
        return carry

    ntrips = jnp.where(ctx_only, 0, (nchunks - 1) // unroll)
    lax.fori_loop(0, ntrips, body, 0)
    last = jnp.where(ctx_only, 0, nchunks - 1)
    for s in range(nsub):
        acc_sc[:, cols(s)] += jnp.dot(vt_tile_fn(last, s), p_sc[:, cols(s)], preferred_element_type=F32)

    finite = jnp.logical_and(jnp.isfinite(jnp.sum(acc_sc[...], axis=0, keepdims=True)),
                             jnp.isfinite(jnp.sum(l_sc[...], axis=0, keepdims=True)))
    overflowed = jnp.max(jnp.where(finite, 0.0, 1.0)) > 0.5

    @pl.when(overflowed)
    def _():
        m_sc[...] = jnp.full(m_sc.shape, NEG_INF, F32)
        l_sc[...] = jnp.zeros(l_sc.shape, F32)
        acc_sc[...] = jnp.zeros(acc_sc.shape, F32)

        def exact(j, carry):
            st = scores(j)
            for s in range(nsub):
                m_old = m_sc[:, cols(s)]
                m_new = jnp.maximum(m_old, jnp.max(st[s], axis=0, keepdims=True))
                alpha = jnp.exp2(m_old - m_new)
                p = jnp.exp2(st[s] - m_new)
                pv = jnp.dot(vt_tile_fn(j, s), p.astype(BF16), preferred_element_type=F32)
                acc_sc[:, cols(s)] = alpha * acc_sc[:, cols(s)] + pv
                l_sc[:, cols(s)] = alpha * l_sc[:, cols(s)] + sublane_sums(p)
                m_sc[:, cols(s)] = m_new
            return carry

        lax.fori_loop(0, last + 1, exact, 0)

    acc = acc_sc[...]
    l = jnp.sum(l_sc[...], axis=0, keepdims=True)
    return [acc[:, cols(s)] / l[:, cols(s)] for s in range(nsub)]


def _flash_scratch(nsub):
    w = nsub * T
    return [pltpu.VMEM((T, w), BF16),
            pltpu.VMEM((1, w), F32),
            pltpu.VMEM((8, w), F32),
            pltpu.VMEM((VT_ROWS, w), F32)]


def _diff_kernel(q_ref, k_ref, vt_ref, qmask_ref, lamp_ref, dnw_ref, o_ref, *scratch):
    ctx_only = pl.program_id(2) == 0
    nchunks = k_ref.shape[1] // T
    qf = q_ref[0].astype(F32)
    qts = [(qf * qmask_ref[s:s + 1, :]).T.astype(BF16) for s in range(4)]

    def k_rows(start, n, s):
        return k_ref[0, pl.ds(start, n), :]

    def vt_tile(j, s):
        return vt_ref[0, s // 2, j]

    outs = _flash_t(qts, k_rows, vt_tile, ctx_only, nchunks, scratch)
    lp = lamp_ref[0]
    lam_init = lp[4:5, 0:1]
    lam = (jnp.exp(jnp.sum(lp[0:1] * lp[1:2], axis=1, keepdims=True))
           - jnp.exp(jnp.sum(lp[2:3] * lp[3:4], axis=1, keepdims=True)) + lam_init)
    ys = []
    for hd in range(2):
        o = outs[2 * hd] - lam * outs[2 * hd + 1]
        ys.append(o * lax.rsqrt(jnp.mean(o * o, axis=0, keepdims=True) + LN_EPS))
    y = jnp.concatenate(ys, axis=0).T
    o_ref[0] = ((y * dnw_ref[0]) * (1.0 - lam_init)).astype(o_ref.dtype)


def _diff_attn(qa, ka, vt, qmask, lamp_l, dnw_l):
    B, S, _ = qa.shape
    nt = S // T
    G = DA_HEADS // 2
    return pl.pallas_call(
        _diff_kernel,
        grid=(B, G, nt),
        in_specs=[pl.BlockSpec((1, T, LANES), lambda b, g, i: (b, i, g)),
                  pl.BlockSpec((1, S, LANES), lambda b, g, i: (b, 0, g)),
                  pl.BlockSpec((1, 2, nt, VT_ROWS, T), lambda b, g, i: (b, MLA_HEADS // 2 + g, 0, 0, 0)),
                  pl.BlockSpec((8, LANES), lambda b, g, i: (0, 0)),
                  pl.BlockSpec((1, 8, DA_QK), lambda b, g, i: (0, 0, 0)),
                  pl.BlockSpec((1, 1, LANES), lambda b, g, i: (0, 0, 0))],
        out_specs=pl.BlockSpec((1, T, LANES), lambda b, g, i: (b, i, g)),
        out_shape=jax.ShapeDtypeStruct((B, S, DA_HEADS * DA_V), BF16),
        scratch_shapes=_flash_scratch(4),
        compiler_params=_cparams(("arbitrary", "arbitrary", "arbitrary")),
        name="diff_attn",
    )(qa, ka, vt, qmask, lamp_l, dnw_l)


def _mla_kernel(q_ref, k_ref, vt_ref, o_ref, *scratch):
    ctx_only = pl.program_id(1) == 0
    nchunks = k_ref.shape[1] // T
    qf = q_ref[0].astype(F32)
    qts = [qf[:, s * LANES:(s + 1) * LANES].T.astype(BF16) for s in range(MLA_HEADS)]

    def k_rows(start, n, s):
        return k_ref[0, pl.ds(start, n), s * LANES:(s + 1) * LANES]

    def vt_tile(j, s):
        return vt_ref[0, s, j]

    outs = _flash_t(qts, k_rows, vt_tile, ctx_only, nchunks, scratch)
    o_ref[0] = jnp.concatenate(outs, axis=0).T.astype(o_ref.dtype)


def _mla_attn(qm, km, vt):
    B, S, W = qm.shape
    nt = S // T
    return pl.pallas_call(
        _mla_kernel,
        grid=(B, nt),
        in_specs=[pl.BlockSpec((1, T, W), lambda b, i: (b, i, 0)),
                  pl.BlockSpec((1, S, W), lambda b, i: (b, 0, 0)),
                  pl.BlockSpec((1, MLA_HEADS, nt, VT_ROWS, T), lambda b, i: (b, 0, 0, 0, 0))],
        out_specs=pl.BlockSpec((1, T, MLA_HEADS * MLA_V), lambda b, i: (b, i, 0)),
        out_shape=jax.ShapeDtypeStruct((B, S, MLA_HEADS * MLA_V), BF16),
        scratch_shapes=_flash_scratch(MLA_HEADS),
        compiler_params=_cparams(("arbitrary", "arbitrary")),
        name="mla_attn",
    )(qm, km, vt)


def _na_kernel(q_ref, k_ref, v_ref, blk_ref, hmask_ref, o_ref, bias_sc, *, ctx_len, rows, nlat):
    i = pl.program_id(2)
    qf = q_ref[0].astype(F32)
    lane_head = hmask_ref[...]
    k_ctx = k_ref[0, 0:ctx_len, :]
    v_ctx = v_ref[0, 0:ctx_len, :]

    @pl.when(i == 0)
    def _():
        blk = _na_block_index(rows)
        left = lax.broadcasted_iota(jnp.int32, (GRID_W, LANES), 1) < GRID_W
        for hd in range(2):
            for v in range(3):
                for a in range(T // GRID_W):
                    for kp in range(NA_WIN_ROWS // 2):
                        tile = jnp.where(left, blk_ref[0, hd * NA_BLOCKS + int(blk[v, a, 2 * kp])],
                                         blk_ref[0, hd * NA_BLOCKS + int(blk[v, a, 2 * kp + 1])])
                        bias_sc[v, hd, a * GRID_W:(a + 1) * GRID_W, kp * LANES:(kp + 1) * LANES] = tile

        out = jnp.zeros((T, LANES), F32)
        for hd in range(2):
            hm = lane_head[hd:hd + 1, :]
            qh = (qf * hm).astype(BF16)
            s = lax.dot_general(qh, k_ctx, _NT, preferred_element_type=F32)
            m = jnp.max(s, axis=1, keepdims=True)
            p = jnp.exp(s - m)
            l = jnp.sum(p, axis=1, keepdims=True)
            o = jnp.dot(p.astype(BF16), v_ctx, preferred_element_type=F32) / l
            out = out + o * hm
        o_ref[0] = out.astype(o_ref.dtype)

    @pl.when(i > 0)
    def _():
        r = (i - 1) * (T // GRID_W)
        start = jnp.clip(r - NA_KH // 2, 0, rows - NA_WIN_ROWS)
        off = pl.multiple_of(ctx_len + start * GRID_W, GRID_W)
        k_win = k_ref[0, pl.ds(off, NA_WIN_ROWS * GRID_W), :]
        v_win = v_ref[0, pl.ds(off, NA_WIN_ROWS * GRID_W), :]
        variant = jnp.where(i == 1, 0, jnp.where(i == nlat, 2, 1))
        out = jnp.zeros((T, LANES), F32)
        for hd in range(2):
            hm = lane_head[hd:hd + 1, :]
            qh = (qf * hm).astype(BF16)
            s_w = lax.dot_general(qh, k_win, _NT, preferred_element_type=F32) + bias_sc[variant, hd]
            s_c = lax.dot_general(qh, k_ctx, _NT, preferred_element_type=F32)
            m = jnp.maximum(jnp.max(s_w, axis=1, keepdims=True), jnp.max(s_c, axis=1, keepdims=True))
            p_w = jnp.exp(s_w - m)
            p_c = jnp.exp(s_c - m)
            l = jnp.sum(p_w, axis=1, keepdims=True) + jnp.sum(p_c, axis=1, keepdims=True)
            o = (jnp.dot(p_w.astype(BF16), v_win, preferred_element_type=F32)
                 + jnp.dot(p_c.astype(BF16), v_ctx, preferred_element_type=F32)) / l
            out = out + o * hm
        o_ref[0] = out.astype(o_ref.dtype)


def _na_attn(qn, kn, vn, blocks_l, hmask, ctx_len):
    B, S, _ = qn.shape
    nt = S // T
    rows = (S - ctx_len) // GRID_W
    G = NA_HEADS // 2
    nlat = nt - ctx_len // T
    return pl.pallas_call(
        functools.partial(_na_kernel, ctx_len=ctx_len, rows=rows, nlat=nlat),
        grid=(B, G, nt),
        in_specs=[pl.BlockSpec((1, T, LANES), lambda b, g, i: (b, i, g)),
                  pl.BlockSpec((1, S, LANES), lambda b, g, i: (b, 0, g)),
                  pl.BlockSpec((1, S, LANES), lambda b, g, i: (b, 0, g)),
                  pl.BlockSpec((1, 2 * NA_BLOCKS, GRID_W, LANES), lambda b, g, i: (0, g, 0, 0)),
                  pl.BlockSpec((8, LANES), lambda b, g, i: (0, 0))],
        out_specs=pl.BlockSpec((1, T, LANES), lambda b, g, i: (b, i, g)),
        out_shape=jax.ShapeDtypeStruct((B, S, NA_HEADS * NA_DIM), BF16),
        scratch_shapes=[pltpu.VMEM((3, 2, T, NA_WIN_ROWS * GRID_W), F32)],
        compiler_params=_cparams(("arbitrary", "arbitrary", "arbitrary")),
        name="na_attn",
    )(qn, kn, vn, blocks_l, hmask)


def _outproj_kernel(x_ref, ma_ref, mb_ref, mc_ref, mod_ref, w_ref, g_ref, b_ref, o_ref, *, alpha):
    D = x_ref.shape[-1]
    mix = jnp.concatenate([ma_ref[0], mb_ref[0], mc_ref[0]], axis=1)
    y = jnp.dot(mix, w_ref[...], preferred_element_type=F32)
    gate = mod_ref[0, 0][:, 2 * D:3 * D]
    o_ref[0] = _layernorm(alpha * x_ref[0] + gate * y) * g_ref[...] + b_ref[...]


def _outproj(xs, mix_a, mix_b, mix_c, mod_tok, w_out, g1, b1, alpha):
    B, S, D = xs.shape
    nt = S // T
    tok = lambda w: pl.BlockSpec((1, T, w), lambda b, i: (b, i, 0))
    full = lambda a: pl.BlockSpec(a.shape, lambda b, i: (0,) * a.ndim)
    return pl.pallas_call(
        functools.partial(_outproj_kernel, alpha=alpha),
        grid=(B, nt),
        in_specs=[tok(D), tok(mix_a.shape[-1]), tok(mix_b.shape[-1]), tok(mix_c.shape[-1]),
                  pl.BlockSpec((1, 1, 1, 6 * D), lambda b, i: (b, jnp.minimum(i, 1), 0, 0)),
                  full(w_out), full(g1), full(b1)],
        out_specs=tok(D),
        out_shape=jax.ShapeDtypeStruct((B, S, D), F32),
        compiler_params=_cparams(("arbitrary", "arbitrary")),
        name="outproj",
    )(xs, mix_a, mix_b, mix_c, mod_tok, w_out, g1, b1)


def _ffn_kernel(x_ref, xp_ref, xn_ref, mod_ref, wu_ref, cw_ref, wd_ref, g_ref, b_ref,
                o_ref, *u_refs, alpha, nlat_first, nlast):
    D = x_ref.shape[-1]
    i = pl.program_id(1)
    mod = mod_ref[0, 0]
    shift, scale, gate = mod[:, 3 * D:4 * D], mod[:, 4 * D:5 * D], mod[:, 5 * D:6 * D]
    x = x_ref[0]
    has_prev = jnp.logical_and(i != 0, i != nlat_first)
    has_next = jnp.logical_and(i != nlat_first - 1, i != nlast)
    hp = (_layernorm(xp_ref[0]) * (1.0 + scale) + shift) * jnp.where(has_prev, 1.0, 0.0)
    hn = (_layernorm(xn_ref[0]) * (1.0 + scale) + shift) * jnp.where(has_next, 1.0, 0.0)
    hc = _layernorm(x) * (1.0 + scale) + shift
    h = jnp.concatenate([hp, hc, hn], axis=0).astype(BF16)
    n = x.shape[0]
    dff = wd_ref.shape[0]
    nck = len(FF_SPLITS)
    edges = [sum(FF_SPLITS[:j]) for j in range(nck + 1)]
    gcols = [slice(edges[j], edges[j + 1]) for j in range(nck)]
    vcols = [slice(dff + edges[j], dff + edges[j + 1]) for j in range(nck)]

    def up(j):
        u_refs[2 * j][...] = jnp.dot(h, wu_ref[:, gcols[j]], preferred_element_type=F32)
        u_refs[2 * j + 1][...] = jnp.dot(h, wu_ref[:, vcols[j]], preferred_element_type=F32)

    def conv(u_ref, c):
        return (u_ref[HALO - 1:HALO - 1 + n, :] * c[0:1] + u_ref[HALO:HALO + n, :] * c[1:2]
                + u_ref[HALO + 1:HALO + 1 + n, :] * c[2:3] + c[3:4])

    def down(j):
        ug = conv(u_refs[2 * j], cw_ref[:, gcols[j]])
        uv = conv(u_refs[2 * j + 1], cw_ref[:, vcols[j]])
        a = ((ug * jax.nn.sigmoid(ug)) * uv).astype(BF16)
        return jnp.dot(a, wd_ref[gcols[j], :], preferred_element_type=F32)

    up(0)
    y = None
    for j in range(nck):
        if j + 1 < nck:
            up(j + 1)
        d = down(j)
        y = d if y is None else y + d
    o_ref[0] = _layernorm(alpha * x + gate * y) * g_ref[...] + b_ref[...]


def _ffn(xs, mod_tok, wu, cw, wd, g2, b2, alpha, ctx_len, latent_only):
    B, S, D = xs.shape
    nt = S // T
    th = T // HALO
    nh = S // HALO
    nctx = ctx_len // T
    tok = pl.BlockSpec((1, T, D), lambda b, i: (b, i, 0))
    full = lambda a: pl.BlockSpec(a.shape, lambda b, i: (0,) * a.ndim)
    if latent_only:
        out_spec = pl.BlockSpec((1, T, D), lambda b, i: (b, jnp.maximum(i - nctx, 0), 0))
        out_rows = S - ctx_len
    else:
        out_spec, out_rows = tok, S
    return pl.pallas_call(
        functools.partial(_ffn_kernel, alpha=alpha, nlat_first=ctx_len // T, nlast=nt - 1),
        grid=(B, nt),
        in_specs=[tok,
                  pl.BlockSpec((1, HALO, D), lambda b, i: (b, jnp.maximum(i * th - 1, 0), 0)),
                  pl.BlockSpec((1, HALO, D), lambda b, i: (b, jnp.minimum((i + 1) * th, nh - 1), 0)),
                  pl.BlockSpec((1, 1, 1, 6 * D), lambda b, i: (b, jnp.minimum(i, 1), 0, 0)),
                  full(wu), full(cw), full(wd), full(g2), full(b2)],
        out_specs=out_spec,
        out_shape=jax.ShapeDtypeStruct((B, out_rows, D), F32),
        scratch_shapes=[pltpu.VMEM((T + 2 * HALO, w), F32) for w in FF_SPLITS for _ in range(2)],
        compiler_params=_cparams(("arbitrary", "arbitrary")),
        name="ffn",
    )(xs, xs, xs, mod_tok, wu, cw, wd, g2, b2)


def _rot_perm(width):
    j = np.arange(width)
    jj = j % (ROPE_DIM // 2)
    lo = jj < ROPE_DIM // 4
    src = np.where(lo, j + ROPE_DIM // 4, j - ROPE_DIM // 4)
    sign = np.where(lo, -1.0, 1.0).astype(np.float32)
    return src, sign


def _rope_tables(n, ctx_len):
    t = jnp.arange(n)
    row = (t // GRID_W).astype(F32)
    col = (t % GRID_W).astype(F32)
    axis_dim = ROPE_DIM // 2
    inv_freq = ROPE_BASE ** (-jnp.arange(0, axis_dim, 2, dtype=F32) / axis_dim)

    def cs(pos):
        ang = pos[:, None] * inv_freq[None, :]
        ang = jnp.concatenate([ang, ang], axis=-1)
        return jnp.cos(ang), jnp.sin(ang)

    cr, sr = cs(row)
    cc, sc = cs(col)
    cos32 = jnp.concatenate([cr, cc], axis=-1)
    sin32 = jnp.concatenate([sr, sc], axis=-1)
    cos32 = jnp.concatenate([jnp.ones((ctx_len, ROPE_DIM), F32), cos32], axis=0)
    sin32 = jnp.concatenate([jnp.zeros((ctx_len, ROPE_DIM), F32), sin32], axis=0)
    S = n + ctx_len
    cosa = jnp.tile(cos32, (1, LANES // ROPE_DIM))
    sina = jnp.tile(sin32, (1, LANES // ROPE_DIM))
    cosm = jnp.concatenate([jnp.ones((S, MLA_NOPE), F32), cos32, jnp.ones((S, LANES - MLA_NOPE - MLA_ROPE), F32)], axis=1)
    sinm = jnp.concatenate([jnp.zeros((S, MLA_NOPE), F32), sin32, jnp.zeros((S, LANES - MLA_NOPE - MLA_ROPE), F32)], axis=1)
    return cosa, sina, cosm, sinm


def _na_block_index(rows):
    qrows = T // GRID_W
    masked = 2 * NA_KH - 1
    blk = np.full((3, qrows, NA_WIN_ROWS), masked, np.int32)
    for v, r in enumerate((0, 2 * qrows, rows - qrows)):
        start = int(np.clip(r - NA_KH // 2, 0, rows - NA_WIN_ROWS))
        for a in range(qrows):
            qr = r + a
            r0 = int(np.clip(qr - NA_KH // 2, 0, rows - NA_KH))
            for kb in range(NA_WIN_ROWS):
                kr = start + kb
                if r0 <= kr < r0 + NA_KH:
                    blk[v, a, kb] = kr - qr + (NA_KH - 1)
    return blk


def _bias_block_kernel(rpb_ref, dc_ref, band_ref, o_ref, *, nh, ndr, ndc):
    l = pl.program_id(0)
    dc = dc_ref[...]
    band = band_ref[...] > 0
    neg = jnp.full(dc.shape, NEG_INF, F32)
    for h in range(nh):
        def one(r, carry, h=h):
            row = (l * nh + h) * ndr + r
            tile = neg
            for d in range(ndc):
                tile = jnp.where(dc == d, rpb_ref[row, d], tile)
            o_ref[0, h * NA_BLOCKS + r] = jnp.where(band, tile, NEG_INF)
            return carry
        lax.fori_loop(0, ndr, one, 0)
        o_ref[0, h * NA_BLOCKS + NA_BLOCKS - 1] = neg


def _na_bias_blocks(na_rpb):
    L, H, ndr, ndc = na_rpb.shape
    assert ndr == NA_BLOCKS - 1
    qc = np.arange(GRID_W)
    c0 = np.clip(qc - NA_KW // 2, 0, GRID_W - NA_KW)
    band = (qc[None, :] >= c0[:, None]) & (qc[None, :] < c0[:, None] + NA_KW)
    dcm = np.clip(qc[None, :] - qc[:, None], -(NA_KW - 1), NA_KW - 1) + (NA_KW - 1)
    dup = lambda a: jnp.asarray(np.concatenate([a, a], axis=1), jnp.int32)
    return pl.pallas_call(
        functools.partial(_bias_block_kernel, nh=H, ndr=ndr, ndc=ndc),
        grid=(L,),
        in_specs=[pl.BlockSpec(memory_space=pltpu.SMEM),
                  pl.BlockSpec((GRID_W, LANES), lambda l: (0, 0)),
                  pl.BlockSpec((GRID_W, LANES), lambda l: (0, 0))],
        out_specs=pl.BlockSpec((1, H * NA_BLOCKS, GRID_W, LANES), lambda l: (l, 0, 0, 0)),
        out_shape=jax.ShapeDtypeStruct((L, H * NA_BLOCKS, GRID_W, LANES), F32),
        compiler_params=_cparams(("arbitrary",)),
        name="na_bias_blocks",
    )(na_rpb.reshape(L * H * ndr, ndc), dup(dcm), dup(band))


def kernel(x, c, ctx, c_ctx, w_ada, b_ada, w_in, lam_q1, lam_k1, lam_q2, lam_k2, diff_norm_w, na_rpb,
           mla_q_norm_w, mla_kv_norm_w, w_uq, w_ukv, w_out, ln1_g, ln1_b, w_up, conv_w, conv_b, w_down,
           ln2_g, ln2_b):
    B, N, D = x.shape
    C = ctx.shape[1]
    L = w_in.shape[0]
    S = C + N
    rows = N // GRID_W
    assert C == T and N % T == 0
    assert rows >= NA_WIN_ROWS and B + 1 <= 8 and sum(FF_SPLITS) == D_FF
    alpha = (2 * L) ** 0.25

    cond8 = jnp.zeros((8, D), F32).at[0:B].set(c).at[B].set(c_ctx)
    mod_all = _ada(cond8, w_ada, b_ada)
    ctx_rows = jnp.broadcast_to(mod_all[:, B][:, None], (L, B, 6 * D))
    mod_tok_all = jnp.stack([ctx_rows, mod_all[:, 0:B]], axis=2).reshape(L, B, 2, 1, 6 * D)

    tabs = _rope_tables(N, C)
    lane = np.arange(LANES)
    qmask = np.zeros((8, LANES), np.float32)
    for s in range(4):
        qmask[s] = (lane // DA_QK == s)
    hmask = np.zeros((8, LANES), np.float32)
    for hd in range(2):
        hmask[hd] = (lane // NA_DIM == hd)
    qmask, hmask = jnp.asarray(qmask), jnp.asarray(hmask)
    bias_blocks = _na_bias_blocks(na_rpb)

    src_a, sgn_a = _rot_perm(DA_HEADS * 2 * DA_QK)
    src_r, sgn_r = _rot_perm(MLA_ROPE)
    sp = np.cumsum([0, 384, 384, 384, 384, 384, 384, MLA_Q_RANK, MLA_KV_RANK, MLA_ROPE])
    zeros = lambda *s: jnp.zeros(s, F32)

    xs = jnp.concatenate([ctx, x], axis=1)
    for l in range(L):
        wi = w_in[l]
        a_q, a_k, a_v, n_q, n_k, n_v, c_q, c_kv, k_r = [wi[:, sp[t]:sp[t + 1]] for t in range(9)]
        kr_pad = jnp.concatenate([zeros(D, MLA_NOPE), k_r, zeros(D, LANES - MLA_NOPE - MLA_ROPE)], axis=1)
        krr_pad = jnp.concatenate([zeros(D, MLA_NOPE), k_r[:, src_r] * sgn_r,
                                   zeros(D, LANES - MLA_NOPE - MLA_ROPE)], axis=1)
        w_ext = jnp.concatenate([a_q, a_q[:, src_a] * sgn_a, a_k, a_k[:, src_a] * sgn_a, n_q, n_k, n_v,
                                 c_q, c_kv, kr_pad, krr_pad], axis=1).astype(BF16)
        wvt = a_v.T.astype(BF16)
        uq = w_uq[l].reshape(MLA_Q_RANK, MLA_HEADS, MLA_NOPE + MLA_ROPE)
        uq_rope = uq[:, :, MLA_NOPE:]
        padq = jnp.zeros((MLA_Q_RANK, MLA_HEADS, LANES - MLA_NOPE - MLA_ROPE), F32)
        wuq = jnp.concatenate([
            jnp.concatenate([uq, padq], axis=2).reshape(MLA_Q_RANK, MLA_HEADS * LANES),
            jnp.concatenate([jnp.zeros_like(uq[:, :, :MLA_NOPE]), uq_rope[:, :, src_r] * sgn_r, padq],
                            axis=2).reshape(MLA_Q_RANK, MLA_HEADS * LANES)], axis=1).astype(BF16)
        ukv = w_ukv[l].reshape(MLA_KV_RANK, MLA_HEADS, MLA_NOPE + MLA_V)
        wukvk = jnp.concatenate([ukv[:, :, :MLA_NOPE], jnp.zeros((MLA_KV_RANK, MLA_HEADS, LANES - MLA_NOPE), F32)],
                                axis=2).reshape(MLA_KV_RANK, MLA_HEADS * LANES).astype(BF16)
        wukvvt = ukv[:, :, MLA_NOPE:].reshape(MLA_KV_RANK, MLA_HEADS * MLA_V).T.astype(BF16)
        gq = mla_q_norm_w[l].reshape(1, MLA_Q_RANK)
        gkv = mla_kv_norm_w[l].reshape(1, MLA_KV_RANK)
        mod_tok = mod_tok_all[l]

        qa, ka, qn, kn, vn, qm, km, vt = _inproj(xs, mod_tok, tabs, (w_ext, wvt, gq, gkv, wuq, wukvk, wukvvt))

        lam_init = 0.8 - 0.6 * math.exp(-0.3 * l)
        lamp = jnp.zeros((1, 8, DA_QK), F32)
        lamp = lamp.at[0, 0].set(lam_q1[l]).at[0, 1].set(lam_k1[l]).at[0, 2].set(lam_q2[l]).at[0, 3].set(lam_k2[l])
        lamp = lamp.at[0, 4].set(lam_init)
        dnw = jnp.concatenate([diff_norm_w[l], diff_norm_w[l]]).reshape(1, 1, LANES)
        mix_a = _diff_attn(qa, ka, vt, qmask, lamp, dnw)

        mix_b = _na_attn(qn, kn, vn, bias_blocks[l:l + 1], hmask, C)
        mix_c = _mla_attn(qm, km, vt)

        x1 = _outproj(xs, mix_a, mix_b, mix_c, mod_tok, w_out[l].astype(BF16),
                      ln1_g[l].reshape(1, D), ln1_b[l].reshape(1, D), alpha)

        cw = jnp.concatenate([conv_w[l], conv_b[l][None], jnp.zeros((4, 2 * D_FF), F32)], axis=0)
        xs = _ffn(x1, mod_tok, w_up[l].astype(BF16), cw, w_down[l].astype(BF16),
                  ln2_g[l].reshape(1, D), ln2_b[l].reshape(1, D), alpha, C, latent_only=(l == L - 1))
    return xs
```

```python
import functools
import math

import numpy as np
import jax
import jax.numpy as jnp
from jax import lax
from jax.experimental import pallas as pl
from jax.experimental.pallas import tpu as pltpu

F32 = jnp.float32
BF16 = jnp.bfloat16

GRID_W = 64
ROPE_DIM = 32
ROPE_BASE = 10000.0
DA_HEADS, DA_QK, DA_V = 6, 32, 64
NA_HEADS, NA_DIM, NA_KH, NA_KW = 6, 64, 8, 16
MLA_HEADS, MLA_Q_RANK, MLA_KV_RANK, MLA_NOPE, MLA_ROPE, MLA_V = 4, 256, 128, 64, 32, 64
D_FF = 2816
LN_EPS = 1e-6
NEG_INF = -1e30
DA_SCALE = DA_QK ** -0.5
NA_SCALE = NA_DIM ** -0.5
MLA_SCALE = (MLA_NOPE + MLA_ROPE) ** -0.5
LOG2E = math.log2(math.e)

T = 256
FLASH_UNROLL = 32
MXU_ORDER = (("qk", 0), ("pv", 0), ("qk", 1), ("pv", 1), ("qk", 2), ("pv", 2), ("qk", 3), ("pv", 3))
LANES = 128
LOW_COLS = MLA_Q_RANK + MLA_KV_RANK + 2 * LANES
VT_ROWS = 64
FF_SPLITS = (1024, 1024, 768)
NA_WIN_ROWS = 12
NA_BLOCKS = 2 * NA_KH
HALO = 8
VMEM_LIMIT = 56 * 1024 * 1024

_NT = (((1,), (1,)), ((), ()))


def _layernorm(x):
    mu = jnp.mean(x, axis=-1, keepdims=True)
    xc = x - mu
    var = jnp.mean(xc * xc, axis=-1, keepdims=True)
    return xc * lax.rsqrt(var + LN_EPS)


def _rms(x, g):
    return (x * lax.rsqrt(jnp.mean(x * x, axis=-1, keepdims=True) + LN_EPS)) * g


def _cparams(sem):
    return pltpu.CompilerParams(dimension_semantics=sem, vmem_limit_bytes=VMEM_LIMIT)


def _ada_kernel(cond_ref, w_ref, b_ref, o_ref):
    c = cond_ref[...]
    a = (c * jax.nn.sigmoid(c)).astype(BF16)
    o_ref[0] = jnp.dot(a, w_ref[0].astype(BF16), preferred_element_type=F32) + b_ref[0]


def _ada(cond8, w_ada, b_ada):
    L, D, N = w_ada.shape
    tn = 1536
    return pl.pallas_call(
        _ada_kernel,
        grid=(L, N // tn),
        in_specs=[pl.BlockSpec((8, D), lambda l, j: (0, 0)),
                  pl.BlockSpec((1, D, tn), lambda l, j: (l, 0, j)),
                  pl.BlockSpec((1, 1, tn), lambda l, j: (l, 0, j))],
        out_specs=pl.BlockSpec((1, 8, tn), lambda l, j: (l, 0, j)),
        out_shape=jax.ShapeDtypeStruct((L, 8, N), F32),
        compiler_params=_cparams(("arbitrary", "arbitrary")),
        name="ada",
    )(cond8, w_ada, b_ada.reshape(L, 1, N))


def _inproj_kernel(x_ref, mod_ref, cosa_ref, sina_ref, cosm_ref, sinm_ref, w_ref, wvt_ref,
                   gq_ref, gkv_ref, wuq_ref, wukvk_ref, wukvvt_ref,
                   qa_ref, ka_ref, qn_ref, kn_ref, vn_ref, qm_ref, km_ref, vt_ref):
    D = x_ref.shape[-1]
    x = x_ref[0]
    mod = mod_ref[0, 0]
    h = (_layernorm(x) * (1.0 + mod[:, D:2 * D]) + mod[:, 0:D]).astype(BF16)
    lo = jnp.dot(h, w_ref[:, 0:LOW_COLS], preferred_element_type=F32)
    vta = lax.dot_general(wvt_ref[...], h, _NT, preferred_element_type=F32)
    hi = jnp.dot(h, w_ref[:, LOW_COLS:], preferred_element_type=F32)
    proj = jnp.concatenate([hi, lo], axis=1)

    cosa, sina = cosa_ref[...], sina_ref[...]
    cos3 = jnp.concatenate([cosa] * 3, axis=1)
    sin3 = jnp.concatenate([sina] * 3, axis=1)
    qa_ref[0] = ((proj[:, 0:384] * cos3 + proj[:, 384:768] * sin3) * (DA_SCALE * LOG2E)).astype(BF16)
    ka_ref[0] = (proj[:, 768:1152] * cos3 + proj[:, 1152:1536] * sin3).astype(BF16)
    qn_ref[0] = (proj[:, 1536:1920] * NA_SCALE).astype(BF16)
    kn_ref[0] = proj[:, 1920:2304].astype(BF16)
    vn_ref[0] = proj[:, 2304:2688].astype(BF16)

    cosm, sinm = cosm_ref[...], sinm_ref[...]
    cos4 = jnp.concatenate([cosm] * 4, axis=1)
    sin4 = jnp.concatenate([sinm] * 4, axis=1)
    cqn = _rms(proj[:, 2688:2944], gq_ref[...]).astype(BF16)
    qc = jnp.dot(cqn, wuq_ref[...], preferred_element_type=F32)
    qm_ref[0] = ((qc[:, 0:512] * cos4 + qc[:, 512:1024] * sin4) * (MLA_SCALE * LOG2E)).astype(BF16)
    kvn = _rms(proj[:, 2944:3072], gkv_ref[...]).astype(BF16)
    kr = proj[:, 3072:3200] * cosm + proj[:, 3200:3328] * sinm
    km = jnp.dot(kvn, wukvk_ref[...], preferred_element_type=F32) + jnp.concatenate([kr] * 4, axis=1)
    km_ref[0] = km.astype(BF16)

    vtm = lax.dot_general(wukvvt_ref[...], kvn, _NT, preferred_element_type=F32)
    for hd in range(MLA_HEADS):
        vt_ref[0, hd, 0] = vtm[hd * MLA_V:(hd + 1) * MLA_V].astype(BF16)
    for hd in range(DA_HEADS):
        vt_ref[0, MLA_HEADS + hd, 0] = vta[hd * DA_V:(hd + 1) * DA_V].astype(BF16)


def _inproj(xs, mod_tok, tabs, wts):
    B, S, D = xs.shape
    nt = S // T
    cosa, sina, cosm, sinm = tabs
    w_ext, wvt, gq, gkv, wuq, wukvk, wukvvt = wts
    tok = lambda w: pl.BlockSpec((1, T, w), lambda b, i: (b, i, 0))
    tab = pl.BlockSpec((T, LANES), lambda b, i: (i, 0))
    full = lambda a: pl.BlockSpec(a.shape, lambda b, i: (0,) * a.ndim)
    nvt = DA_HEADS + MLA_HEADS
    out_shape = [jax.ShapeDtypeStruct((B, S, w), BF16) for w in (384, 384, 384, 384, 384, 512, 512)]
    out_shape.append(jax.ShapeDtypeStruct((B, nvt, nt, VT_ROWS, T), BF16))
    out_specs = [tok(w) for w in (384, 384, 384, 384, 384, 512, 512)]
    out_specs.append(pl.BlockSpec((1, nvt, 1, VT_ROWS, T), lambda b, i: (b, 0, i, 0, 0)))
    return pl.pallas_call(
        _inproj_kernel,
        grid=(B, nt),
        in_specs=[tok(D),
                  pl.BlockSpec((1, 1, 1, 6 * D), lambda b, i: (b, jnp.minimum(i, 1), 0, 0)),
                  tab, tab, tab, tab,
                  full(w_ext), full(wvt), full(gq), full(gkv), full(wuq), full(wukvk), full(wukvvt)],
        out_specs=out_specs,
        out_shape=out_shape,
        compiler_params=_cparams(("arbitrary", "arbitrary")),
        name="inproj",
    )(xs, mod_tok, cosa, sina, cosm, sinm, w_ext, wvt, gq, gkv, wuq, wukvk, wukvvt)


def _flash_t(qts, k_rows_fn, vt_tile_fn, ctx_only, nchunks, scratch, finish):
    p_sc, m_sc, l_sc, acc_sc = scratch
    nsub = len(qts)
    unroll = math.gcd(nchunks - 1, FLASH_UNROLL)

    def cols(s):
        return slice(s * T, (s + 1) * T)

    def scores(j):
        start = pl.multiple_of(j * T, T)
        return [jnp.dot(k_rows_fn(start, T, s), qts[s], preferred_element_type=F32) for s in range(nsub)]

    def sublane_sums(p):
        return jnp.sum(p.reshape(T // 8, 8, p.shape[1]), axis=0)

    st0 = scores(0)
    for s in range(nsub):
        m0 = jnp.max(st0[s], axis=0, keepdims=True)
        p0 = jnp.exp2(st0[s] - m0)
        m_sc[:, cols(s)] = m0
        l_sc[:, cols(s)] = sublane_sums(p0)
        p_sc[:, cols(s)] = p0.astype(BF16)
    acc_sc[...] = jnp.zeros(acc_sc.shape, F32)

    def body(i, carry):
        pvs, ls, p_prev = [None] * nsub, [None] * nsub, None
        for c in range(unroll):
            j = 1 + i * unroll + c
            start = pl.multiple_of(j * T, T)
            st = [None] * nsub
            for kind, s in MXU_ORDER:
                if kind == "qk":
                    st[s] = jnp.dot(k_rows_fn(start, T, s), qts[s], preferred_element_type=F32)
                else:
                    p = p_sc[:, cols(s)] if c == 0 else p_prev[s]
                    d = jnp.dot(vt_tile_fn(j - 1, s), p, preferred_element_type=F32)
                    pvs[s] = d if c == 0 else pvs[s] + d
            p_prev = []
            for s in range(nsub):
                p = jnp.exp2(st[s] - m_sc[:, cols(s)])
                ls[s] = sublane_sums(p) if c == 0 else ls[s] + sublane_sums(p)
                p_prev.append(p.astype(BF16))
        for s in range(nsub):
            p_sc[:, cols(s)] = p_prev[s]
            l_sc[:, cols(s)] += ls[s]
            acc_sc[:, cols(s)] += pvs[s]
        return carry

    ntrips = jnp.where(ctx_only, 0, (nchunks - 1) // unroll)
    lax.fori_loop(0, ntrips, body, 0)
    last = jnp.where(ctx_only, 0, nchunks - 1)
    for s in range(nsub):
        acc_sc[:, cols(s)] += jnp.dot(vt_tile_fn(last, s), p_sc[:, cols(s)], preferred_element_type=F32)

    def normalised():
        acc = acc_sc[...]
        l = jnp.sum(l_sc[...], axis=0, keepdims=True)
        return [acc[:, cols(s)] / l[:, cols(s)] for s in range(nsub)]

    finite = jnp.logical_and(jnp.isfinite(jnp.sum(acc_sc[...], axis=0, keepdims=True)),
                             jnp.isfinite(jnp.sum(l_sc[...], axis=0, keepdims=True)))
    overflowed = jnp.max(jnp.where(finite, 0.0, 1.0)) > 0.5
    finish(normalised())

    @pl.when(overflowed)
    def _():
        m_sc[...] = jnp.full(m_sc.shape, NEG_INF, F32)
        l_sc[...] = jnp.zeros(l_sc.shape, F32)
        acc_sc[...] = jnp.zeros(acc_sc.shape, F32)

        def exact(j, carry):
            st = scores(j)
            for s in range(nsub):
                m_old = m_sc[:, cols(s)]
                m_new = jnp.maximum(m_old, jnp.max(st[s], axis=0, keepdims=True))
                alpha = jnp.exp2(m_old - m_new)
                p = jnp.exp2(st[s] - m_new)
                pv = jnp.dot(vt_tile_fn(j, s), p.astype(BF16), preferred_element_type=F32)
                acc_sc[:, cols(s)] = alpha * acc_sc[:, cols(s)] + pv
                l_sc[:, cols(s)] = alpha * l_sc[:, cols(s)] + sublane_sums(p)
                m_sc[:, cols(s)] = m_new
            return carry

        lax.fori_loop(0, last + 1, exact, 0)
        finish(normalised())


def _flash_scratch(nsub):
    w = nsub * T
    return [pltpu.VMEM((T, w), BF16),
            pltpu.VMEM((1, w), F32),
            pltpu.VMEM((8, w), F32),
            pltpu.VMEM((VT_ROWS, w), F32)]


def _diff_kernel(q_ref, k_ref, vt_ref, lamp_ref, dnw_ref, o_ref, *scratch):
    ctx_only = pl.program_id(2) == 0
    nchunks = k_ref.shape[1] // T
    qf = q_ref[0].astype(F32)
    qt = qf.T
    grp = lax.broadcasted_iota(jnp.int32, qt.shape, 0) // DA_QK
    qts = [jnp.where(grp == s, qt, 0.0).astype(BF16) for s in range(4)]

    def k_rows(start, n, s):
        return k_ref[0, pl.ds(start, n), :]

    def vt_tile(j, s):
        return vt_ref[0, s // 2, j]

    def finish(outs):
        lp = lamp_ref[0]
        lam_init = lp[4:5, 0:1]
        lam = (jnp.exp(jnp.sum(lp[0:1] * lp[1:2], axis=1, keepdims=True))
               - jnp.exp(jnp.sum(lp[2:3] * lp[3:4], axis=1, keepdims=True)) + lam_init)
        ys = []
        for hd in range(2):
            o = outs[2 * hd] - lam * outs[2 * hd + 1]
            ys.append(o * lax.rsqrt(jnp.mean(o * o, axis=0, keepdims=True) + LN_EPS))
        y = jnp.concatenate(ys, axis=0).T
        o_ref[0] = ((y * dnw_ref[0]) * (1.0 - lam_init)).astype(o_ref.dtype)

    _flash_t(qts, k_rows, vt_tile, ctx_only, nchunks, scratch, finish)


def _diff_attn(qa, ka, vt, lamp_l, dnw_l):
    B, S, _ = qa.shape
    nt = S // T
    G = DA_HEADS // 2
    return pl.pallas_call(
        _diff_kernel,
        grid=(B, G, nt),
        in_specs=[pl.BlockSpec((1, T, LANES), lambda b, g, i: (b, i, g)),
                  pl.BlockSpec((1, S, LANES), lambda b, g, i: (b, 0, g)),
                  pl.BlockSpec((1, 2, nt, VT_ROWS, T), lambda b, g, i: (b, MLA_HEADS // 2 + g, 0, 0, 0)),
                  pl.BlockSpec((1, 8, DA_QK), lambda b, g, i: (0, 0, 0)),
                  pl.BlockSpec((1, 1, LANES), lambda b, g, i: (0, 0, 0))],
        out_specs=pl.BlockSpec((1, T, LANES), lambda b, g, i: (b, i, g)),
        out_shape=jax.ShapeDtypeStruct((B, S, DA_HEADS * DA_V), BF16),
        scratch_shapes=_flash_scratch(4),
        compiler_params=_cparams(("arbitrary", "arbitrary", "arbitrary")),
        name="diff_attn",
    )(qa, ka, vt, lamp_l, dnw_l)


def _mla_kernel(q_ref, k_ref, vt_ref, o_ref, *scratch):
    ctx_only = pl.program_id(1) == 0
    nchunks = k_ref.shape[1] // T
    qf = q_ref[0].astype(F32)
    qts = [qf[:, s * LANES:(s + 1) * LANES].T.astype(BF16) for s in range(MLA_HEADS)]

    def k_rows(start, n, s):
        return k_ref[0, pl.ds(start, n), s * LANES:(s + 1) * LANES]

    def vt_tile(j, s):
        return vt_ref[0, s, j]

    def finish(outs):
        o_ref[0] = jnp.concatenate(outs, axis=0).T.astype(o_ref.dtype)

    _flash_t(qts, k_rows, vt_tile, ctx_only, nchunks, scratch, finish)


def _mla_attn(qm, km, vt):
    B, S, W = qm.shape
    nt = S // T
    return pl.pallas_call(
        _mla_kernel,
        grid=(B, nt),
        in_specs=[pl.BlockSpec((1, T, W), lambda b, i: (b, i, 0)),
                  pl.BlockSpec((1, S, W), lambda b, i: (b, 0, 0)),
                  pl.BlockSpec((1, MLA_HEADS, nt, VT_ROWS, T), lambda b, i: (b, 0, 0, 0, 0))],
        out_specs=pl.BlockSpec((1, T, MLA_HEADS * MLA_V), lambda b, i: (b, i, 0)),
        out_shape=jax.ShapeDtypeStruct((B, S, MLA_HEADS * MLA_V), BF16),
        scratch_shapes=_flash_scratch(MLA_HEADS),
        compiler_params=_cparams(("arbitrary", "arbitrary")),
        name="mla_attn",
    )(qm, km, vt)


def _na_kernel(q_ref, k_ref, v_ref, blk_ref, hmask_ref, o_ref, bias_sc, *, ctx_len, rows, nlat):
    i = pl.program_id(2)
    qf = q_ref[0].astype(F32)
    lane_head = hmask_ref[...]
    k_ctx = k_ref[0, 0:ctx_len, :]
    v_ctx = v_ref[0, 0:ctx_len, :]

    @pl.when(i == 0)
    def _():
        blk = _na_block_index(rows)
        left = lax.broadcasted_iota(jnp.int32, (GRID_W, LANES), 1) < GRID_W
        for hd in range(2):
            for v in range(3):
                for a in range(T // GRID_W):
                    for kp in range(NA_WIN_ROWS // 2):
                        tile = jnp.where(left, blk_ref[0, hd * NA_BLOCKS + int(blk[v, a, 2 * kp])],
                                         blk_ref[0, hd * NA_BLOCKS + int(blk[v, a, 2 * kp + 1])])
                        bias_sc[v, hd, a * GRID_W:(a + 1) * GRID_W, kp * LANES:(kp + 1) * LANES] = tile

        out = jnp.zeros((T, LANES), F32)
        for hd in range(2):
            hm = lane_head[hd:hd + 1, :]
            qh = (qf * hm).astype(BF16)
            s = lax.dot_general(qh, k_ctx, _NT, preferred_element_type=F32)
            m = jnp.max(s, axis=1, keepdims=True)
            p = jnp.exp(s - m)
            l = jnp.sum(p, axis=1, keepdims=True)
            o = jnp.dot(p.astype(BF16), v_ctx, preferred_element_type=F32) / l
            out = out + o * hm
        o_ref[0] = out.astype(o_ref.dtype)

    @pl.when(i > 0)
    def _():
        r = (i - 1) * (T // GRID_W)
        start = jnp.clip(r - NA_KH // 2, 0, rows - NA_WIN_ROWS)
        off = pl.multiple_of(ctx_len + start * GRID_W, GRID_W)
        k_win = k_ref[0, pl.ds(off, NA_WIN_ROWS * GRID_W), :]
        v_win = v_ref[0, pl.ds(off, NA_WIN_ROWS * GRID_W), :]
        variant = jnp.where(i == 1, 0, jnp.where(i == nlat, 2, 1))
        out = jnp.zeros((T, LANES), F32)
        for hd in range(2):
            hm = lane_head[hd:hd + 1, :]
            qh = (qf * hm).astype(BF16)
            s_w = lax.dot_general(qh, k_win, _NT, preferred_element_type=F32) + bias_sc[variant, hd]
            s_c = lax.dot_general(qh, k_ctx, _NT, preferred_element_type=F32)
            m = jnp.maximum(jnp.max(s_w, axis=1, keepdims=True), jnp.max(s_c, axis=1, keepdims=True))
            p_w = jnp.exp(s_w - m)
            p_c = jnp.exp(s_c - m)
            l = jnp.sum(p_w, axis=1, keepdims=True) + jnp.sum(p_c, axis=1, keepdims=True)
            o = (jnp.dot(p_w.astype(BF16), v_win, preferred_element_type=F32)
                 + jnp.dot(p_c.astype(BF16), v_ctx, preferred_element_type=F32)) / l
            out = out + o * hm
        o_ref[0] = out.astype(o_ref.dtype)


def _na_attn(qn, kn, vn, blocks_l, hmask, ctx_len):
    B, S, _ = qn.shape
    nt = S // T
    rows = (S - ctx_len) // GRID_W
    G = NA_HEADS // 2
    nlat = nt - ctx_len // T
    return pl.pallas_call(
        functools.partial(_na_kernel, ctx_len=ctx_len, rows=rows, nlat=nlat),
        grid=(B, G, nt),
        in_specs=[pl.BlockSpec((1, T, LANES), lambda b, g, i: (b, i, g)),
                  pl.BlockSpec((1, S, LANES), lambda b, g, i: (b, 0, g)),
                  pl.BlockSpec((1, S, LANES), lambda b, g, i: (b, 0, g)),
                  pl.BlockSpec((1, 2 * NA_BLOCKS, GRID_W, LANES), lambda b, g, i: (0, g, 0, 0)),
                  pl.BlockSpec((8, LANES), lambda b, g, i: (0, 0))],
        out_specs=pl.BlockSpec((1, T, LANES), lambda b, g, i: (b, i, g)),
        out_shape=jax.ShapeDtypeStruct((B, S, NA_HEADS * NA_DIM), BF16),
        scratch_shapes=[pltpu.VMEM((3, 2, T, NA_WIN_ROWS * GRID_W), F32)],
        compiler_params=_cparams(("arbitrary", "arbitrary", "arbitrary")),
        name="na_attn",
    )(qn, kn, vn, blocks_l, hmask)


def _outproj_kernel(x_ref, ma_ref, mb_ref, mc_ref, mod_ref, w_ref, g_ref, b_ref, o_ref, *, alpha):
    D = x_ref.shape[-1]
    mix = jnp.concatenate([ma_ref[0], mb_ref[0], mc_ref[0]], axis=1)
    y = jnp.dot(mix, w_ref[...], preferred_element_type=F32)
    gate = mod_ref[0, 0][:, 2 * D:3 * D]
    o_ref[0] = _layernorm(alpha * x_ref[0] + gate * y) * g_ref[...] + b_ref[...]


def _outproj(xs, mix_a, mix_b, mix_c, mod_tok, w_out, g1, b1, alpha):
    B, S, D = xs.shape
    nt = S // T
    tok = lambda w: pl.BlockSpec((1, T, w), lambda b, i: (b, i, 0))
    full = lambda a: pl.BlockSpec(a.shape, lambda b, i: (0,) * a.ndim)
    return pl.pallas_call(
        functools.partial(_outproj_kernel, alpha=alpha),
        grid=(B, nt),
        in_specs=[tok(D), tok(mix_a.shape[-1]), tok(mix_b.shape[-1]), tok(mix_c.shape[-1]),
                  pl.BlockSpec((1, 1, 1, 6 * D), lambda b, i: (b, jnp.minimum(i, 1), 0, 0)),
                  full(w_out), full(g1), full(b1)],
        out_specs=tok(D),
        out_shape=jax.ShapeDtypeStruct((B, S, D), F32),
        compiler_params=_cparams(("arbitrary", "arbitrary")),
        name="outproj",
    )(xs, mix_a, mix_b, mix_c, mod_tok, w_out, g1, b1)


def _ffn_kernel(x_ref, xp_ref, xn_ref, mod_ref, wu_ref, cw_ref, wd_ref, g_ref, b_ref,
                o_ref, *u_refs, alpha, nlat_first, nlast):
    D = x_ref.shape[-1]
    i = pl.program_id(1)
    mod = mod_ref[0, 0]
    shift, scale, gate = mod[:, 3 * D:4 * D], mod[:, 4 * D:5 * D], mod[:, 5 * D:6 * D]
    x = x_ref[0]
    has_prev = jnp.logical_and(i != 0, i != nlat_first)
    has_next = jnp.logical_and(i != nlat_first - 1, i != nlast)
    hp = (_layernorm(xp_ref[0]) * (1.0 + scale) + shift) * jnp.where(has_prev, 1.0, 0.0)
    hn = (_layernorm(xn_ref[0]) * (1.0 + scale) + shift) * jnp.where(has_next, 1.0, 0.0)
    hc = _layernorm(x) * (1.0 + scale) + shift
    h = jnp.concatenate([hp, hc, hn], axis=0).astype(BF16)
    n = x.shape[0]
    dff = wd_ref.shape[0]
    nck = len(FF_SPLITS)
    edges = [sum(FF_SPLITS[:j]) for j in range(nck + 1)]
    gcols = [slice(edges[j], edges[j + 1]) for j in range(nck)]
    vcols = [slice(dff + edges[j], dff + edges[j + 1]) for j in range(nck)]

    def up(j):
        u_refs[2 * j][...] = jnp.dot(h, wu_ref[:, gcols[j]], preferred_element_type=F32)
        u_refs[2 * j + 1][...] = jnp.dot(h, wu_ref[:, vcols[j]], preferred_element_type=F32)

    def conv(u_ref, c):
        return (u_ref[HALO - 1:HALO - 1 + n, :] * c[0:1] + u_ref[HALO:HALO + n, :] * c[1:2]
                + u_ref[HALO + 1:HALO + 1 + n, :] * c[2:3] + c[3:4])

    def down(j):
        ug = conv(u_refs[2 * j], cw_ref[:, gcols[j]])
        uv = conv(u_refs[2 * j + 1], cw_ref[:, vcols[j]])
        a = ((ug * jax.nn.sigmoid(ug)) * uv).astype(BF16)
        return jnp.dot(a, wd_ref[gcols[j], :], preferred_element_type=F32)

    up(0)
    y = None
    for j in range(nck):
        if j + 1 < nck:
            up(j + 1)
        d = down(j)
        y = d if y is None else y + d
    o_ref[0] = _layernorm(alpha * x + gate * y) * g_ref[...] + b_ref[...]


def _ffn(xs, mod_tok, wu, cw, wd, g2, b2, alpha, ctx_len, latent_only):
    B, S, D = xs.shape
    nt = S // T
    th = T // HALO
    nh = S // HALO
    nctx = ctx_len // T
    tok = pl.BlockSpec((1, T, D), lambda b, i: (b, i, 0))
    full = lambda a: pl.BlockSpec(a.shape, lambda b, i: (0,) * a.ndim)
    if latent_only:
        out_spec = pl.BlockSpec((1, T, D), lambda b, i: (b, jnp.maximum(i - nctx, 0), 0))
        out_rows = S - ctx_len
    else:
        out_spec, out_rows = tok, S
    return pl.pallas_call(
        functools.partial(_ffn_kernel, alpha=alpha, nlat_first=ctx_len // T, nlast=nt - 1),
        grid=(B, nt),
        in_specs=[tok,
                  pl.BlockSpec((1, HALO, D), lambda b, i: (b, jnp.maximum(i * th - 1, 0), 0)),
                  pl.BlockSpec((1, HALO, D), lambda b, i: (b, jnp.minimum((i + 1) * th, nh - 1), 0)),
                  pl.BlockSpec((1, 1, 1, 6 * D), lambda b, i: (b, jnp.minimum(i, 1), 0, 0)),
                  full(wu), full(cw), full(wd), full(g2), full(b2)],
        out_specs=out_spec,
        out_shape=jax.ShapeDtypeStruct((B, out_rows, D), F32),
        scratch_shapes=[pltpu.VMEM((T + 2 * HALO, w), F32) for w in FF_SPLITS for _ in range(2)],
        compiler_params=_cparams(("arbitrary", "arbitrary")),
        name="ffn",
    )(xs, xs, xs, mod_tok, wu, cw, wd, g2, b2)


def _rot_perm(width):
    j = np.arange(width)
    jj = j % (ROPE_DIM // 2)
    lo = jj < ROPE_DIM // 4
    src = np.where(lo, j + ROPE_DIM // 4, j - ROPE_DIM // 4)
    sign = np.where(lo, -1.0, 1.0).astype(np.float32)
    return src, sign


def _rope_tables(n, ctx_len):
    t = jnp.arange(n)
    row = (t // GRID_W).astype(F32)
    col = (t % GRID_W).astype(F32)
    axis_dim = ROPE_DIM // 2
    inv_freq = ROPE_BASE ** (-jnp.arange(0, axis_dim, 2, dtype=F32) / axis_dim)

    def cs(pos):
        ang = pos[:, None] * inv_freq[None, :]
        ang = jnp.concatenate([ang, ang], axis=-1)
        return jnp.cos(ang), jnp.sin(ang)

    cr, sr = cs(row)
    cc, sc = cs(col)
    cos32 = jnp.concatenate([cr, cc], axis=-1)
    sin32 = jnp.concatenate([sr, sc], axis=-1)
    cos32 = jnp.concatenate([jnp.ones((ctx_len, ROPE_DIM), F32), cos32], axis=0)
    sin32 = jnp.concatenate([jnp.zeros((ctx_len, ROPE_DIM), F32), sin32], axis=0)
    S = n + ctx_len
    cosa = jnp.tile(cos32, (1, LANES // ROPE_DIM))
    sina = jnp.tile(sin32, (1, LANES // ROPE_DIM))
    cosm = jnp.concatenate([jnp.ones((S, MLA_NOPE), F32), cos32, jnp.ones((S, LANES - MLA_NOPE - MLA_ROPE), F32)], axis=1)
    sinm = jnp.concatenate([jnp.zeros((S, MLA_NOPE), F32), sin32, jnp.zeros((S, LANES - MLA_NOPE - MLA_ROPE), F32)], axis=1)
    return cosa, sina, cosm, sinm


def _na_block_index(rows):
    qrows = T // GRID_W
    masked = 2 * NA_KH - 1
    blk = np.full((3, qrows, NA_WIN_ROWS), masked, np.int32)
    for v, r in enumerate((0, 2 * qrows, rows - qrows)):
        start = int(np.clip(r - NA_KH // 2, 0, rows - NA_WIN_ROWS))
        for a in range(qrows):
            qr = r + a
            r0 = int(np.clip(qr - NA_KH // 2, 0, rows - NA_KH))
            for kb in range(NA_WIN_ROWS):
                kr = start + kb
                if r0 <= kr < r0 + NA_KH:
                    blk[v, a, kb] = kr - qr + (NA_KH - 1)
    return blk


def _bias_block_kernel(rpb_ref, dc_ref, band_ref, o_ref, *, nh, ndr, ndc):
    l = pl.program_id(0)
    dc = dc_ref[...]
    band = band_ref[...] > 0
    neg = jnp.full(dc.shape, NEG_INF, F32)
    for h in range(nh):
        def one(r, carry, h=h):
            row = (l * nh + h) * ndr + r
            tile = neg
            for d in range(ndc):
                tile = jnp.where(dc == d, rpb_ref[row, d], tile)
            o_ref[0, h * NA_BLOCKS + r] = jnp.where(band, tile, NEG_INF)
            return carry
        lax.fori_loop(0, ndr, one, 0)
        o_ref[0, h * NA_BLOCKS + NA_BLOCKS - 1] = neg


def _na_bias_blocks(na_rpb):
    L, H, ndr, ndc = na_rpb.shape
    assert ndr == NA_BLOCKS - 1
    qc = np.arange(GRID_W)
    c0 = np.clip(qc - NA_KW // 2, 0, GRID_W - NA_KW)
    band = (qc[None, :] >= c0[:, None]) & (qc[None, :] < c0[:, None] + NA_KW)
    dcm = np.clip(qc[None, :] - qc[:, None], -(NA_KW - 1), NA_KW - 1) + (NA_KW - 1)
    dup = lambda a: jnp.asarray(np.concatenate([a, a], axis=1), jnp.int32)
    return pl.pallas_call(
        functools.partial(_bias_block_kernel, nh=H, ndr=ndr, ndc=ndc),
        grid=(L,),
        in_specs=[pl.BlockSpec(memory_space=pltpu.SMEM),
                  pl.BlockSpec((GRID_W, LANES), lambda l: (0, 0)),
                  pl.BlockSpec((GRID_W, LANES), lambda l: (0, 0))],
        out_specs=pl.BlockSpec((1, H * NA_BLOCKS, GRID_W, LANES), lambda l: (l, 0, 0, 0)),
        out_shape=jax.ShapeDtypeStruct((L, H * NA_BLOCKS, GRID_W, LANES), F32),
        compiler_params=_cparams(("arbitrary",)),
        name="na_bias_blocks",
    )(na_rpb.reshape(L * H * ndr, ndc), dup(dcm), dup(band))


def kernel(x, c, ctx, c_ctx, w_ada, b_ada, w_in, lam_q1, lam_k1, lam_q2, lam_k2, diff_norm_w, na_rpb,
           mla_q_norm_w, mla_kv_norm_w, w_uq, w_ukv, w_out, ln1_g, ln1_b, w_up, conv_w, conv_b, w_down,
           ln2_g, ln2_b):
    B, N, D = x.shape
    C = ctx.shape[1]
    L = w_in.shape[0]
    S = C + N
    rows = N // GRID_W
    assert C == T and N % T == 0
    assert rows >= NA_WIN_ROWS and B + 1 <= 8 and sum(FF_SPLITS) == D_FF
    alpha = (2 * L) ** 0.25

    cond8 = jnp.zeros((8, D), F32).at[0:B].set(c).at[B].set(c_ctx)
    mod_all = _ada(cond8, w_ada, b_ada)
    ctx_rows = jnp.broadcast_to(mod_all[:, B][:, None], (L, B, 6 * D))
    mod_tok_all = jnp.stack([ctx_rows, mod_all[:, 0:B]], axis=2).reshape(L, B, 2, 1, 6 * D)

    tabs = _rope_tables(N, C)
    lane = np.arange(LANES)
    hmask = np.zeros((8, LANES), np.float32)
    for hd in range(2):
        hmask[hd] = (lane // NA_DIM == hd)
    hmask = jnp.asarray(hmask)
    bias_blocks = _na_bias_blocks(na_rpb)

    src_a, sgn_a = _rot_perm(DA_HEADS * 2 * DA_QK)
    src_r, sgn_r = _rot_perm(MLA_ROPE)
    sp = np.cumsum([0, 384, 384, 384, 384, 384, 384, MLA_Q_RANK, MLA_KV_RANK, MLA_ROPE])
    zeros = lambda *s: jnp.zeros(s, F32)

    xs = jnp.concatenate([ctx, x], axis=1)
    for l in range(L):
        wi = w_in[l]
        a_q, a_k, a_v, n_q, n_k, n_v, c_q, c_kv, k_r = [wi[:, sp[t]:sp[t + 1]] for t in range(9)]
        kr_pad = jnp.concatenate([zeros(D, MLA_NOPE), k_r, zeros(D, LANES - MLA_NOPE - MLA_ROPE)], axis=1)
        krr_pad = jnp.concatenate([zeros(D, MLA_NOPE), k_r[:, src_r] * sgn_r,
                                   zeros(D, LANES - MLA_NOPE - MLA_ROPE)], axis=1)
        w_ext = jnp.concatenate([c_q, c_kv, kr_pad, krr_pad,
                                 a_q, a_q[:, src_a] * sgn_a, a_k, a_k[:, src_a] * sgn_a, n_q, n_k, n_v],
                                axis=1).astype(BF16)
        wvt = a_v.T.astype(BF16)
        uq = w_uq[l].reshape(MLA_Q_RANK, MLA_HEADS, MLA_NOPE + MLA_ROPE)
        uq_rope = uq[:, :, MLA_NOPE:]
        padq = jnp.zeros((MLA_Q_RANK, MLA_HEADS, LANES - MLA_NOPE - MLA_ROPE), F32)
        wuq = jnp.concatenate([
            jnp.concatenate([uq, padq], axis=2).reshape(MLA_Q_RANK, MLA_HEADS * LANES),
            jnp.concatenate([jnp.zeros_like(uq[:, :, :MLA_NOPE]), uq_rope[:, :, src_r] * sgn_r, padq],
                            axis=2).reshape(MLA_Q_RANK, MLA_HEADS * LANES)], axis=1).astype(BF16)
        ukv = w_ukv[l].reshape(MLA_KV_RANK, MLA_HEADS, MLA_NOPE + MLA_V)
        wukvk = jnp.concatenate([ukv[:, :, :MLA_NOPE], jnp.zeros((MLA_KV_RANK, MLA_HEADS, LANES - MLA_NOPE), F32)],
                                axis=2).reshape(MLA_KV_RANK, MLA_HEADS * LANES).astype(BF16)
        wukvvt = ukv[:, :, MLA_NOPE:].reshape(MLA_KV_RANK, MLA_HEADS * MLA_V).T.astype(BF16)
        gq = mla_q_norm_w[l].reshape(1, MLA_Q_RANK)
        gkv = mla_kv_norm_w[l].reshape(1, MLA_KV_RANK)
        mod_tok = mod_tok_all[l]

        qa, ka, qn, kn, vn, qm, km, vt = _inproj(xs, mod_tok, tabs, (w_ext, wvt, gq, gkv, wuq, wukvk, wukvvt))

        lam_init = 0.8 - 0.6 * math.exp(-0.3 * l)
        lamp = jnp.zeros((1, 8, DA_QK), F32)
        lamp = lamp.at[0, 0].set(lam_q1[l]).at[0, 1].set(lam_k1[l]).at[0, 2].set(lam_q2[l]).at[0, 3].set(lam_k2[l])
        lamp = lamp.at[0, 4].set(lam_init)
        dnw = jnp.concatenate([diff_norm_w[l], diff_norm_w[l]]).reshape(1, 1, LANES)
        mix_a = _diff_attn(qa, ka, vt, lamp, dnw)

        mix_b = _na_attn(qn, kn, vn, bias_blocks[l:l + 1], hmask, C)
        mix_c = _mla_attn(qm, km, vt)

        x1 = _outproj(xs, mix_a, mix_b, mix_c, mod_tok, w_out[l].astype(BF16),
                      ln1_g[l].reshape(1, D), ln1_b[l].reshape(1, D), alpha)

        cw = jnp.concatenate([conv_w[l], conv_b[l][None], jnp.zeros((4, 2 * D_FF), F32)], axis=0)
        xs = _ffn(x1, mod_tok, w_up[l].astype(BF16), cw, w_down[l].astype(BF16),
                  ln2_g[l].reshape(1, D), ln2_b[l].reshape(1, D), alpha, C, latent_only=(l == L - 1))
    return xs
```

```python
import functools
import math

import numpy as np
import jax
import jax.numpy as jnp
from jax import lax
from jax.experimental import pallas as pl
from jax.experimental.pallas import tpu as pltpu

F32 = jnp.float32
BF16 = jnp.bfloat16

GRID_W = 64
ROPE_DIM = 32
ROPE_BASE = 10000.0
DA_HEADS, DA_QK, DA_V = 6, 32, 64
NA_HEADS, NA_DIM, NA_KH, NA_KW = 6, 64, 8, 16
MLA_HEADS, MLA_Q_RANK, MLA_KV_RANK, MLA_NOPE, MLA_ROPE, MLA_V = 4, 256, 128, 64, 32, 64
D_FF = 2816
LN_EPS = 1e-6
NEG_INF = -1e30
DA_SCALE = DA_QK ** -0.5
NA_SCALE = NA_DIM ** -0.5
MLA_SCALE = (MLA_NOPE + MLA_ROPE) ** -0.5
LOG2E = math.log2(math.e)

T = 256
FLASH_UNROLL = 32
MXU_ORDER = (("qk", 0), ("pv", 0), ("qk", 1), ("pv", 1), ("qk", 2), ("pv", 2), ("qk", 3), ("pv", 3))
LANES = 128
LOW_COLS = MLA_Q_RANK + MLA_KV_RANK + 2 * LANES
VT_ROWS = 64
FF_SPLITS = (1024, 1024, 768)
NA_WIN_ROWS = 12
NA_BLOCKS = 2 * NA_KH
HALO = 8
MIX_HALO = 16
VMEM_LIMIT = 56 * 1024 * 1024

_NT = (((1,), (1,)), ((), ()))


def _layernorm(x):
    mu = jnp.mean(x, axis=-1, keepdims=True)
    xc = x - mu
    var = jnp.mean(xc * xc, axis=-1, keepdims=True)
    return xc * lax.rsqrt(var + LN_EPS)


def _rms(x, g):
    return (x * lax.rsqrt(jnp.mean(x * x, axis=-1, keepdims=True) + LN_EPS)) * g


def _cparams(sem):
    return pltpu.CompilerParams(dimension_semantics=sem, vmem_limit_bytes=VMEM_LIMIT)


def _ada_kernel(cond_ref, w_ref, b_ref, o_ref):
    c = cond_ref[...]
    a = (c * jax.nn.sigmoid(c)).astype(BF16)
    o_ref[0] = jnp.dot(a, w_ref[0].astype(BF16), preferred_element_type=F32) + b_ref[0]


def _ada(cond8, w_ada, b_ada):
    L, D, N = w_ada.shape
    tn = 1536
    return pl.pallas_call(
        _ada_kernel,
        grid=(L, N // tn),
        in_specs=[pl.BlockSpec((8, D), lambda l, j: (0, 0)),
                  pl.BlockSpec((1, D, tn), lambda l, j: (l, 0, j)),
                  pl.BlockSpec((1, 1, tn), lambda l, j: (l, 0, j))],
        out_specs=pl.BlockSpec((1, 8, tn), lambda l, j: (l, 0, j)),
        out_shape=jax.ShapeDtypeStruct((L, 8, N), F32),
        compiler_params=_cparams(("arbitrary", "arbitrary")),
        name="ada",
    )(cond8, w_ada, b_ada.reshape(L, 1, N))


def _inproj_kernel(x_ref, mod_ref, cosa_ref, sina_ref, cosm_ref, sinm_ref, w_ref, wvt_ref,
                   gq_ref, gkv_ref, wuq_ref, wukvk_ref, wukvvt_ref,
                   qa_ref, ka_ref, qn_ref, kn_ref, vn_ref, qm_ref, km_ref, vt_ref):
    D = x_ref.shape[-1]
    x = x_ref[0]
    mod = mod_ref[0, 0]
    h = (_layernorm(x) * (1.0 + mod[:, D:2 * D]) + mod[:, 0:D]).astype(BF16)
    lo = jnp.dot(h, w_ref[:, 0:LOW_COLS], preferred_element_type=F32)
    vta = lax.dot_general(wvt_ref[...], h, _NT, preferred_element_type=F32)
    hi = jnp.dot(h, w_ref[:, LOW_COLS:], preferred_element_type=F32)
    proj = jnp.concatenate([hi, lo], axis=1)

    cosa, sina = cosa_ref[...], sina_ref[...]
    cos3 = jnp.concatenate([cosa] * 3, axis=1)
    sin3 = jnp.concatenate([sina] * 3, axis=1)
    qa_ref[0] = ((proj[:, 0:384] * cos3 + proj[:, 384:768] * sin3) * (DA_SCALE * LOG2E)).astype(BF16)
    ka_ref[0] = (proj[:, 768:1152] * cos3 + proj[:, 1152:1536] * sin3).astype(BF16)
    qn_ref[0] = (proj[:, 1536:1920] * NA_SCALE).astype(BF16)
    kn_ref[0] = proj[:, 1920:2304].astype(BF16)
    vn_ref[0] = proj[:, 2304:2688].astype(BF16)

    cosm, sinm = cosm_ref[...], sinm_ref[...]
    cos4 = jnp.concatenate([cosm] * 4, axis=1)
    sin4 = jnp.concatenate([sinm] * 4, axis=1)
    cqn = _rms(proj[:, 2688:2944], gq_ref[...]).astype(BF16)
    qc = jnp.dot(cqn, wuq_ref[...], preferred_element_type=F32)
    qm_ref[0] = ((qc[:, 0:512] * cos4 + qc[:, 512:1024] * sin4) * (MLA_SCALE * LOG2E)).astype(BF16)
    kvn = _rms(proj[:, 2944:3072], gkv_ref[...]).astype(BF16)
    kr = proj[:, 3072:3200] * cosm + proj[:, 3200:3328] * sinm
    km = jnp.dot(kvn, wukvk_ref[...], preferred_element_type=F32) + jnp.concatenate([kr] * 4, axis=1)
    km_ref[0] = km.astype(BF16)

    vtm = lax.dot_general(wukvvt_ref[...], kvn, _NT, preferred_element_type=F32)
    for hd in range(MLA_HEADS):
        vt_ref[0, hd, 0] = vtm[hd * MLA_V:(hd + 1) * MLA_V].astype(BF16)
    for hd in range(DA_HEADS):
        vt_ref[0, MLA_HEADS + hd, 0] = vta[hd * DA_V:(hd + 1) * DA_V].astype(BF16)


def _inproj(xs, mod_tok, tabs, wts):
    B, S, D = xs.shape
    nt = S // T
    cosa, sina, cosm, sinm = tabs
    w_ext, wvt, gq, gkv, wuq, wukvk, wukvvt = wts
    tok = lambda w: pl.BlockSpec((1, T, w), lambda b, i: (b, i, 0))
    tab = pl.BlockSpec((T, LANES), lambda b, i: (i, 0))
    full = lambda a: pl.BlockSpec(a.shape, lambda b, i: (0,) * a.ndim)
    nvt = DA_HEADS + MLA_HEADS
    out_shape = [jax.ShapeDtypeStruct((B, S, w), BF16) for w in (384, 384, 384, 384, 384, 512, 512)]
    out_shape.append(jax.ShapeDtypeStruct((B, nvt, nt, VT_ROWS, T), BF16))
    out_specs = [tok(w) for w in (384, 384, 384, 384, 384, 512, 512)]
    out_specs.append(pl.BlockSpec((1, nvt, 1, VT_ROWS, T), lambda b, i: (b, 0, i, 0, 0)))
    return pl.pallas_call(
        _inproj_kernel,
        grid=(B, nt),
        in_specs=[tok(D),
                  pl.BlockSpec((1, 1, 1, 6 * D), lambda b, i: (b, jnp.minimum(i, 1), 0, 0)),
                  tab, tab, tab, tab,
                  full(w_ext), full(wvt), full(gq), full(gkv), full(wuq), full(wukvk), full(wukvvt)],
        out_specs=out_specs,
        out_shape=out_shape,
        compiler_params=_cparams(("arbitrary", "arbitrary")),
        name="inproj",
    )(xs, mod_tok, cosa, sina, cosm, sinm, w_ext, wvt, gq, gkv, wuq, wukvk, wukvvt)


def _flash_t(qts, k_rows_fn, vt_tile_fn, ctx_only, nchunks, scratch, finish):
    p_sc, m_sc, l_sc, acc_sc = scratch
    nsub = len(qts)

    def cols(s):
        return slice(s * T, (s + 1) * T)

    def chunk_start(j):
        return j * T if isinstance(j, int) else pl.multiple_of(j * T, T)

    def scores(j):
        return [jnp.dot(k_rows_fn(chunk_start(j), T, s), qts[s], preferred_element_type=F32) for s in range(nsub)]

    def sublane_sums(p):
        return jnp.sum(p.reshape(T // 8, 8, p.shape[1]), axis=0)

    def normalised():
        acc = acc_sc[...]
        l = jnp.sum(l_sc[...], axis=0, keepdims=True)
        return [acc[:, cols(s)] / l[:, cols(s)] for s in range(nsub)]

    def attend(nch):
        st0 = scores(0)
        for s in range(nsub):
            m0 = jnp.max(st0[s], axis=0, keepdims=True)
            p0 = jnp.exp2(st0[s] - m0)
            m_sc[:, cols(s)] = m0
            l_sc[:, cols(s)] = sublane_sums(p0)
            p_sc[:, cols(s)] = p0.astype(BF16)
        acc_sc[...] = jnp.zeros(acc_sc.shape, F32)

        unroll = math.gcd(nch - 1, FLASH_UNROLL) if nch > 1 else 1

        def body(i, carry):
            pvs, ls, p_prev = [None] * nsub, [None] * nsub, None
            for c in range(unroll):
                j = 1 + i * unroll + c
                st = [None] * nsub
                for kind, s in MXU_ORDER:
                    if kind == "qk":
                        st[s] = jnp.dot(k_rows_fn(chunk_start(j), T, s), qts[s], preferred_element_type=F32)
                    else:
                        p = p_sc[:, cols(s)] if c == 0 else p_prev[s]
                        d = jnp.dot(vt_tile_fn(j - 1, s), p, preferred_element_type=F32)
                        pvs[s] = d if c == 0 else pvs[s] + d
                p_prev = []
                for s in range(nsub):
                    p = jnp.exp2(st[s] - m_sc[:, cols(s)])
                    ls[s] = sublane_sums(p) if c == 0 else ls[s] + sublane_sums(p)
                    p_prev.append(p.astype(BF16))
            for s in range(nsub):
                p_sc[:, cols(s)] = p_prev[s]
                l_sc[:, cols(s)] += ls[s]
                acc_sc[:, cols(s)] += pvs[s]
            return carry

        ntrips = (nch - 1) // unroll
        if ntrips == 1:
            body(0, 0)
        elif ntrips > 1:
            lax.fori_loop(0, ntrips, body, 0)
        for s in range(nsub):
            acc_sc[:, cols(s)] += jnp.dot(vt_tile_fn(nch - 1, s), p_sc[:, cols(s)], preferred_element_type=F32)

        finite = jnp.logical_and(jnp.isfinite(jnp.sum(acc_sc[...], axis=0, keepdims=True)),
                                 jnp.isfinite(jnp.sum(l_sc[...], axis=0, keepdims=True)))
        overflowed = jnp.max(jnp.where(finite, 0.0, 1.0)) > 0.5
        finish(normalised())

        @pl.when(overflowed)
        def _():
            m_sc[...] = jnp.full(m_sc.shape, NEG_INF, F32)
            l_sc[...] = jnp.zeros(l_sc.shape, F32)
            acc_sc[...] = jnp.zeros(acc_sc.shape, F32)

            def exact(j, carry):
                st = scores(j)
                for s in range(nsub):
                    m_old = m_sc[:, cols(s)]
                    m_new = jnp.maximum(m_old, jnp.max(st[s], axis=0, keepdims=True))
                    alpha = jnp.exp2(m_old - m_new)
                    p = jnp.exp2(st[s] - m_new)
                    pv = jnp.dot(vt_tile_fn(j, s), p.astype(BF16), preferred_element_type=F32)
                    acc_sc[:, cols(s)] = alpha * acc_sc[:, cols(s)] + pv
                    l_sc[:, cols(s)] = alpha * l_sc[:, cols(s)] + sublane_sums(p)
                    m_sc[:, cols(s)] = m_new
                return carry

            lax.fori_loop(0, nch, exact, 0)
            finish(normalised())

    pl.when(ctx_only)(lambda: attend(1))
    pl.when(jnp.logical_not(ctx_only))(lambda: attend(nchunks))


def _flash_scratch(nsub):
    w = nsub * T
    return [pltpu.VMEM((T, w), BF16),
            pltpu.VMEM((1, w), F32),
            pltpu.VMEM((8, w), F32),
            pltpu.VMEM((VT_ROWS, w), F32)]


def _diff_kernel(q_ref, k_ref, vt_ref, lamp_ref, dnw_ref, o_ref, *scratch):
    ctx_only = pl.program_id(2) == 0
    nchunks = k_ref.shape[1] // T
    qf = q_ref[0].astype(F32)
    qt = qf.T
    grp = lax.broadcasted_iota(jnp.int32, qt.shape, 0) // DA_QK
    qts = [jnp.where(grp == s, qt, 0.0).astype(BF16) for s in range(4)]

    def k_rows(start, n, s):
        return k_ref[0, pl.ds(start, n), :]

    def vt_tile(j, s):
        return vt_ref[0, s // 2, j]

    def finish(outs):
        lp = lamp_ref[0]
        lam_init = lp[4:5, 0:1]
        lam = (jnp.exp(jnp.sum(lp[0:1] * lp[1:2], axis=1, keepdims=True))
               - jnp.exp(jnp.sum(lp[2:3] * lp[3:4], axis=1, keepdims=True)) + lam_init)
        ys = []
        for hd in range(2):
            o = outs[2 * hd] - lam * outs[2 * hd + 1]
            ys.append(o * lax.rsqrt(jnp.mean(o * o, axis=0, keepdims=True) + LN_EPS))
        y = jnp.concatenate(ys, axis=0).T
        o_ref[0] = ((y * dnw_ref[0]) * (1.0 - lam_init)).astype(o_ref.dtype)

    _flash_t(qts, k_rows, vt_tile, ctx_only, nchunks, scratch, finish)


def _diff_attn(qa, ka, vt, lamp_l, dnw_l):
    B, S, _ = qa.shape
    nt = S // T
    G = DA_HEADS // 2
    return pl.pallas_call(
        _diff_kernel,
        grid=(B, G, nt),
        in_specs=[pl.BlockSpec((1, T, LANES), lambda b, g, i: (b, i, g)),
                  pl.BlockSpec((1, S, LANES), lambda b, g, i: (b, 0, g)),
                  pl.BlockSpec((1, 2, nt, VT_ROWS, T), lambda b, g, i: (b, MLA_HEADS // 2 + g, 0, 0, 0)),
                  pl.BlockSpec((1, 8, DA_QK), lambda b, g, i: (0, 0, 0)),
                  pl.BlockSpec((1, 1, LANES), lambda b, g, i: (0, 0, 0))],
        out_specs=pl.BlockSpec((1, T, LANES), lambda b, g, i: (b, i, g)),
        out_shape=jax.ShapeDtypeStruct((B, S, DA_HEADS * DA_V), BF16),
        scratch_shapes=_flash_scratch(4),
        compiler_params=_cparams(("arbitrary", "arbitrary", "arbitrary")),
        name="diff_attn",
    )(qa, ka, vt, lamp_l, dnw_l)


def _mla_kernel(q_ref, k_ref, vt_ref, o_ref, *scratch):
    ctx_only = pl.program_id(1) == 0
    nchunks = k_ref.shape[1] // T
    qf = q_ref[0].astype(F32)
    qts = [qf[:, s * LANES:(s + 1) * LANES].T.astype(BF16) for s in range(MLA_HEADS)]

    def k_rows(start, n, s):
        return k_ref[0, pl.ds(start, n), s * LANES:(s + 1) * LANES]

    def vt_tile(j, s):
        return vt_ref[0, s, j]

    def finish(outs):
        o_ref[0] = jnp.concatenate(outs, axis=0).T.astype(o_ref.dtype)

    _flash_t(qts, k_rows, vt_tile, ctx_only, nchunks, scratch, finish)


def _mla_attn(qm, km, vt):
    B, S, W = qm.shape
    nt = S // T
    return pl.pallas_call(
        _mla_kernel,
        grid=(B, nt),
        in_specs=[pl.BlockSpec((1, T, W), lambda b, i: (b, i, 0)),
                  pl.BlockSpec((1, S, W), lambda b, i: (b, 0, 0)),
                  pl.BlockSpec((1, MLA_HEADS, nt, VT_ROWS, T), lambda b, i: (b, 0, 0, 0, 0))],
        out_specs=pl.BlockSpec((1, T, MLA_HEADS * MLA_V), lambda b, i: (b, i, 0)),
        out_shape=jax.ShapeDtypeStruct((B, S, MLA_HEADS * MLA_V), BF16),
        scratch_shapes=_flash_scratch(MLA_HEADS),
        compiler_params=_cparams(("arbitrary", "arbitrary")),
        name="mla_attn",
    )(qm, km, vt)


def _na_kernel(q_ref, k_ref, v_ref, blk_ref, hmask_ref, o_ref, bias_sc, *, ctx_len, rows, nlat):
    i = pl.program_id(2)
    qf = q_ref[0].astype(F32)
    lane_head = hmask_ref[...]
    k_ctx = k_ref[0, 0:ctx_len, :]
    v_ctx = v_ref[0, 0:ctx_len, :]

    @pl.when(i == 0)
    def _():
        blk = _na_block_index(rows)
        left = lax.broadcasted_iota(jnp.int32, (GRID_W, LANES), 1) < GRID_W
        for hd in range(2):
            for v in range(3):
                for a in range(T // GRID_W):
                    for kp in range(NA_WIN_ROWS // 2):
                        tile = jnp.where(left, blk_ref[0, hd * NA_BLOCKS + int(blk[v, a, 2 * kp])],
                                         blk_ref[0, hd * NA_BLOCKS + int(blk[v, a, 2 * kp + 1])])
                        bias_sc[v, hd, a * GRID_W:(a + 1) * GRID_W, kp * LANES:(kp + 1) * LANES] = tile

        out = jnp.zeros((T, LANES), F32)
        for hd in range(2):
            hm = lane_head[hd:hd + 1, :]
            qh = (qf * hm).astype(BF16)
            s = lax.dot_general(qh, k_ctx, _NT, preferred_element_type=F32)
            m = jnp.max(s, axis=1, keepdims=True)
            p = jnp.exp(s - m)
            l = jnp.sum(p, axis=1, keepdims=True)
            o = jnp.dot(p.astype(BF16), v_ctx, preferred_element_type=F32) / l
            out = out + o * hm
        o_ref[0] = out.astype(o_ref.dtype)

    @pl.when(i > 0)
    def _():
        r = (i - 1) * (T // GRID_W)
        start = jnp.clip(r - NA_KH // 2, 0, rows - NA_WIN_ROWS)
        off = pl.multiple_of(ctx_len + start * GRID_W, GRID_W)
        k_win = k_ref[0, pl.ds(off, NA_WIN_ROWS * GRID_W), :]
        v_win = v_ref[0, pl.ds(off, NA_WIN_ROWS * GRID_W), :]
        variant = jnp.where(i == 1, 0, jnp.where(i == nlat, 2, 1))
        out = jnp.zeros((T, LANES), F32)
        for hd in range(2):
            hm = lane_head[hd:hd + 1, :]
            qh = (qf * hm).astype(BF16)
            s_w = lax.dot_general(qh, k_win, _NT, preferred_element_type=F32) + bias_sc[variant, hd]
            s_c = lax.dot_general(qh, k_ctx, _NT, preferred_element_type=F32)
            m = jnp.maximum(jnp.max(s_w, axis=1, keepdims=True), jnp.max(s_c, axis=1, keepdims=True))
            p_w = jnp.exp(s_w - m)
            p_c = jnp.exp(s_c - m)
            l = jnp.sum(p_w, axis=1, keepdims=True) + jnp.sum(p_c, axis=1, keepdims=True)
            o = (jnp.dot(p_w.astype(BF16), v_win, preferred_element_type=F32)
                 + jnp.dot(p_c.astype(BF16), v_ctx, preferred_element_type=F32)) / l
            out = out + o * hm
        o_ref[0] = out.astype(o_ref.dtype)


def _na_attn(qn, kn, vn, blocks_l, hmask, ctx_len):
    B, S, _ = qn.shape
    nt = S // T
    rows = (S - ctx_len) // GRID_W
    G = NA_HEADS // 2
    nlat = nt - ctx_len // T
    return pl.pallas_call(
        functools.partial(_na_kernel, ctx_len=ctx_len, rows=rows, nlat=nlat),
        grid=(B, G, nt),
        in_specs=[pl.BlockSpec((1, T, LANES), lambda b, g, i: (b, i, g)),
                  pl.BlockSpec((1, S, LANES), lambda b, g, i: (b, 0, g)),
                  pl.BlockSpec((1, S, LANES), lambda b, g, i: (b, 0, g)),
                  pl.BlockSpec((1, 2 * NA_BLOCKS, GRID_W, LANES), lambda b, g, i: (0, g, 0, 0)),
                  pl.BlockSpec((8, LANES), lambda b, g, i: (0, 0))],
        out_specs=pl.BlockSpec((1, T, LANES), lambda b, g, i: (b, i, g)),
        out_shape=jax.ShapeDtypeStruct((B, S, NA_HEADS * NA_DIM), BF16),
        scratch_shapes=[pltpu.VMEM((3, 2, T, NA_WIN_ROWS * GRID_W), F32)],
        compiler_params=_cparams(("arbitrary", "arbitrary", "arbitrary")),
        name="na_attn",
    )(qn, kn, vn, blocks_l, hmask)


def _ffn_kernel(x_ref, xp_ref, xn_ref, ma_ref, mb_ref, mc_ref, map_ref, mbp_ref, mcp_ref, man_ref, mbn_ref, mcn_ref,
                mod_ref, wo_ref, g1_ref, b1_ref, wu_ref, cw_ref, wd_ref, g_ref, b_ref,
                o_ref, *u_refs, alpha, nlat_first, nlast):
    D = x_ref.shape[-1]
    i = pl.program_id(1)
    mod = mod_ref[0, 0]
    gate1 = mod[:, 2 * D:3 * D]
    shift, scale, gate = mod[:, 3 * D:4 * D], mod[:, 4 * D:5 * D], mod[:, 5 * D:6 * D]
    mix = jnp.concatenate([jnp.concatenate([a[0], b[0], c[0]], axis=1)
                           for a, b, c in ((map_ref, mbp_ref, mcp_ref), (ma_ref, mb_ref, mc_ref),
                                           (man_ref, mbn_ref, mcn_ref))], axis=0)
    xe = jnp.concatenate([xp_ref[0], x_ref[0], xn_ref[0]], axis=0)
    y1 = jnp.dot(mix, wo_ref[...], preferred_element_type=F32)
    x1e = _layernorm(alpha * xe + gate1 * y1) * g1_ref[...] + b1_ref[...]
    x = x1e[MIX_HALO:MIX_HALO + T]
    has_prev = jnp.logical_and(i != 0, i != nlat_first)
    has_next = jnp.logical_and(i != nlat_first - 1, i != nlast)
    hp = (_layernorm(x1e[MIX_HALO - HALO:MIX_HALO]) * (1.0 + scale) + shift) * jnp.where(has_prev, 1.0, 0.0)
    hn = (_layernorm(x1e[MIX_HALO + T:MIX_HALO + T + HALO]) * (1.0 + scale) + shift) * jnp.where(has_next, 1.0, 0.0)
    hc = _layernorm(x) * (1.0 + scale) + shift
    h = jnp.concatenate([hp, hc, hn], axis=0).astype(BF16)
    n = x.shape[0]
    dff = wd_ref.shape[0]
    nck = len(FF_SPLITS)
    edges = [sum(FF_SPLITS[:j]) for j in range(nck + 1)]
    gcols = [slice(edges[j], edges[j + 1]) for j in range(nck)]
    vcols = [slice(dff + edges[j], dff + edges[j + 1]) for j in range(nck)]

    def up(j):
        u_refs[2 * j][...] = jnp.dot(h, wu_ref[:, gcols[j]], preferred_element_type=F32)
        u_refs[2 * j + 1][...] = jnp.dot(h, wu_ref[:, vcols[j]], preferred_element_type=F32)

    def conv(u_ref, c):
        return (u_ref[HALO - 1:HALO - 1 + n, :] * c[0:1] + u_ref[HALO:HALO + n, :] * c[1:2]
                + u_ref[HALO + 1:HALO + 1 + n, :] * c[2:3] + c[3:4])

    def down(j):
        ug = conv(u_refs[2 * j], cw_ref[:, gcols[j]])
        uv = conv(u_refs[2 * j + 1], cw_ref[:, vcols[j]])
        a = ((ug * jax.nn.sigmoid(ug)) * uv).astype(BF16)
        return jnp.dot(a, wd_ref[gcols[j], :], preferred_element_type=F32)

    up(0)
    y = None
    for j in range(nck):
        if j + 1 < nck:
            up(j + 1)
        d = down(j)
        y = d if y is None else y + d
    o_ref[0] = _layernorm(alpha * x + gate * y) * g_ref[...] + b_ref[...]


def _outproj_ffn(xs, mixes, mod_tok, wo, g1, b1, wu, cw, wd, g2, b2, alpha, ctx_len, latent_only):
    B, S, D = xs.shape
    nt = S // T
    th = T // MIX_HALO
    nh = S // MIX_HALO
    nctx = ctx_len // T
    tok = pl.BlockSpec((1, T, D), lambda b, i: (b, i, 0))
    full = lambda a: pl.BlockSpec(a.shape, lambda b, i: (0,) * a.ndim)
    tokw = lambda w: pl.BlockSpec((1, T, w), lambda b, i: (b, i, 0))
    prevw = lambda w: pl.BlockSpec((1, MIX_HALO, w), lambda b, i: (b, jnp.maximum(i * th - 1, 0), 0))
    nextw = lambda w: pl.BlockSpec((1, MIX_HALO, w), lambda b, i: (b, jnp.minimum((i + 1) * th, nh - 1), 0))
    widths = [m.shape[-1] for m in mixes]
    if latent_only:
        out_spec = pl.BlockSpec((1, T, D), lambda b, i: (b, jnp.maximum(i - nctx, 0), 0))
        out_rows = S - ctx_len
    else:
        out_spec, out_rows = tok, S
    return pl.pallas_call(
        functools.partial(_ffn_kernel, alpha=alpha, nlat_first=ctx_len // T, nlast=nt - 1),
        grid=(B, nt),
        in_specs=[tok, prevw(D), nextw(D),
                  *[tokw(w) for w in widths], *[prevw(w) for w in widths], *[nextw(w) for w in widths],
                  pl.BlockSpec((1, 1, 1, 6 * D), lambda b, i: (b, jnp.minimum(i, 1), 0, 0)),
                  full(wo), full(g1), full(b1), full(wu), full(cw), full(wd), full(g2), full(b2)],
        out_specs=out_spec,
        out_shape=jax.ShapeDtypeStruct((B, out_rows, D), F32),
        scratch_shapes=[pltpu.VMEM((T + 2 * HALO, w), F32) for w in FF_SPLITS for _ in range(2)],
        compiler_params=_cparams(("arbitrary", "arbitrary")),
        name="outproj_ffn",
    )(xs, xs, xs, *mixes, *mixes, *mixes, mod_tok, wo, g1, b1, wu, cw, wd, g2, b2)


def _rot_perm(width):
    j = np.arange(width)
    jj = j % (ROPE_DIM // 2)
    lo = jj < ROPE_DIM // 4
    src = np.where(lo, j + ROPE_DIM // 4, j - ROPE_DIM // 4)
    sign = np.where(lo, -1.0, 1.0).astype(np.float32)
    return src, sign


def _rope_tables(n, ctx_len):
    t = jnp.arange(n)
    row = (t // GRID_W).astype(F32)
    col = (t % GRID_W).astype(F32)
    axis_dim = ROPE_DIM // 2
    inv_freq = ROPE_BASE ** (-jnp.arange(0, axis_dim, 2, dtype=F32) / axis_dim)

    def cs(pos):
        ang = pos[:, None] * inv_freq[None, :]
        ang = jnp.concatenate([ang, ang], axis=-1)
        return jnp.cos(ang), jnp.sin(ang)

    cr, sr = cs(row)
    cc, sc = cs(col)
    cos32 = jnp.concatenate([cr, cc], axis=-1)
    sin32 = jnp.concatenate([sr, sc], axis=-1)
    cos32 = jnp.concatenate([jnp.ones((ctx_len, ROPE_DIM), F32), cos32], axis=0)
    sin32 = jnp.concatenate([jnp.zeros((ctx_len, ROPE_DIM), F32), sin32], axis=0)
    S = n + ctx_len
    cosa = jnp.tile(cos32, (1, LANES // ROPE_DIM))
    sina = jnp.tile(sin32, (1, LANES // ROPE_DIM))
    cosm = jnp.concatenate([jnp.ones((S, MLA_NOPE), F32), cos32, jnp.ones((S, LANES - MLA_NOPE - MLA_ROPE), F32)], axis=1)
    sinm = jnp.concatenate([jnp.zeros((S, MLA_NOPE), F32), sin32, jnp.zeros((S, LANES - MLA_NOPE - MLA_ROPE), F32)], axis=1)
    return cosa, sina, cosm, sinm


def _na_block_index(rows):
    qrows = T // GRID_W
    masked = 2 * NA_KH - 1
    blk = np.full((3, qrows, NA_WIN_ROWS), masked, np.int32)
    for v, r in enumerate((0, 2 * qrows, rows - qrows)):
        start = int(np.clip(r - NA_KH // 2, 0, rows - NA_WIN_ROWS))
        for a in range(qrows):
            qr = r + a
            r0 = int(np.clip(qr - NA_KH // 2, 0, rows - NA_KH))
            for kb in range(NA_WIN_ROWS):
                kr = start + kb
                if r0 <= kr < r0 + NA_KH:
                    blk[v, a, kb] = kr - qr + (NA_KH - 1)
    return blk


def _bias_block_kernel(rpb_ref, dc_ref, band_ref, o_ref, *, nh, ndr, ndc):
    l = pl.program_id(0)
    dc = dc_ref[...]
    band = band_ref[...] > 0
    neg = jnp.full(dc.shape, NEG_INF, F32)
    for h in range(nh):
        def one(r, carry, h=h):
            row = (l * nh + h) * ndr + r
            tile = neg
            for d in range(ndc):
                tile = jnp.where(dc == d, rpb_ref[row, d], tile)
            o_ref[0, h * NA_BLOCKS + r] = jnp.where(band, tile, NEG_INF)
            return carry
        lax.fori_loop(0, ndr, one, 0)
        o_ref[0, h * NA_BLOCKS + NA_BLOCKS - 1] = neg


def _na_bias_blocks(na_rpb):
    L, H, ndr, ndc = na_rpb.shape
    assert ndr == NA_BLOCKS - 1
    qc = np.arange(GRID_W)
    c0 = np.clip(qc - NA_KW // 2, 0, GRID_W - NA_KW)
    band = (qc[None, :] >= c0[:, None]) & (qc[None, :] < c0[:, None] + NA_KW)
    dcm = np.clip(qc[None, :] - qc[:, None], -(NA_KW - 1), NA_KW - 1) + (NA_KW - 1)
    dup = lambda a: jnp.asarray(np.concatenate([a, a], axis=1), jnp.int32)
    return pl.pallas_call(
        functools.partial(_bias_block_kernel, nh=H, ndr=ndr, ndc=ndc),
        grid=(L,),
        in_specs=[pl.BlockSpec(memory_space=pltpu.SMEM),
                  pl.BlockSpec((GRID_W, LANES), lambda l: (0, 0)),
                  pl.BlockSpec((GRID_W, LANES), lambda l: (0, 0))],
        out_specs=pl.BlockSpec((1, H * NA_BLOCKS, GRID_W, LANES), lambda l: (l, 0, 0, 0)),
        out_shape=jax.ShapeDtypeStruct((L, H * NA_BLOCKS, GRID_W, LANES), F32),
        compiler_params=_cparams(("arbitrary",)),
        name="na_bias_blocks",
    )(na_rpb.reshape(L * H * ndr, ndc), dup(dcm), dup(band))


def kernel(x, c, ctx, c_ctx, w_ada, b_ada, w_in, lam_q1, lam_k1, lam_q2, lam_k2, diff_norm_w, na_rpb,
           mla_q_norm_w, mla_kv_norm_w, w_uq, w_ukv, w_out, ln1_g, ln1_b, w_up, conv_w, conv_b, w_down,
           ln2_g, ln2_b):
    B, N, D = x.shape
    C = ctx.shape[1]
    L = w_in.shape[0]
    S = C + N
    rows = N // GRID_W
    assert C == T and N % T == 0
    assert rows >= NA_WIN_ROWS and B + 1 <= 8 and sum(FF_SPLITS) == D_FF
    alpha = (2 * L) ** 0.25

    cond8 = jnp.zeros((8, D), F32).at[0:B].set(c).at[B].set(c_ctx)
    mod_all = _ada(cond8, w_ada, b_ada)
    ctx_rows = jnp.broadcast_to(mod_all[:, B][:, None], (L, B, 6 * D))
    mod_tok_all = jnp.stack([ctx_rows, mod_all[:, 0:B]], axis=2).reshape(L, B, 2, 1, 6 * D)

    tabs = _rope_tables(N, C)
    lane = np.arange(LANES)
    hmask = np.zeros((8, LANES), np.float32)
    for hd in range(2):
        hmask[hd] = (lane // NA_DIM == hd)
    hmask = jnp.asarray(hmask)
    bias_blocks = _na_bias_blocks(na_rpb)

    src_a, sgn_a = _rot_perm(DA_HEADS * 2 * DA_QK)
    src_r, sgn_r = _rot_perm(MLA_ROPE)
    sp = np.cumsum([0, 384, 384, 384, 384, 384, 384, MLA_Q_RANK, MLA_KV_RANK, MLA_ROPE])
    zeros = lambda *s: jnp.zeros(s, F32)

    xs = jnp.concatenate([ctx, x], axis=1)
    for l in range(L):
        wi = w_in[l]
        a_q, a_k, a_v, n_q, n_k, n_v, c_q, c_kv, k_r = [wi[:, sp[t]:sp[t + 1]] for t in range(9)]
        kr_pad = jnp.concatenate([zeros(D, MLA_NOPE), k_r, zeros(D, LANES - MLA_NOPE - MLA_ROPE)], axis=1)
        krr_pad = jnp.concatenate([zeros(D, MLA_NOPE), k_r[:, src_r] * sgn_r,
                                   zeros(D, LANES - MLA_NOPE - MLA_ROPE)], axis=1)
        w_ext = jnp.concatenate([c_q, c_kv, kr_pad, krr_pad,
                                 a_q, a_q[:, src_a] * sgn_a, a_k, a_k[:, src_a] * sgn_a, n_q, n_k, n_v],
                                axis=1).astype(BF16)
        wvt = a_v.T.astype(BF16)
        uq = w_uq[l].reshape(MLA_Q_RANK, MLA_HEADS, MLA_NOPE + MLA_ROPE)
        uq_rope = uq[:, :, MLA_NOPE:]
        padq = jnp.zeros((MLA_Q_RANK, MLA_HEADS, LANES - MLA_NOPE - MLA_ROPE), F32)
        wuq = jnp.concatenate([
            jnp.concatenate([uq, padq], axis=2).reshape(MLA_Q_RANK, MLA_HEADS * LANES),
            jnp.concatenate([jnp.zeros_like(uq[:, :, :MLA_NOPE]), uq_rope[:, :, src_r] * sgn_r, padq],
                            axis=2).reshape(MLA_Q_RANK, MLA_HEADS * LANES)], axis=1).astype(BF16)
        ukv = w_ukv[l].reshape(MLA_KV_RANK, MLA_HEADS, MLA_NOPE + MLA_V)
        wukvk = jnp.concatenate([ukv[:, :, :MLA_NOPE], jnp.zeros((MLA_KV_RANK, MLA_HEADS, LANES - MLA_NOPE), F32)],
                                axis=2).reshape(MLA_KV_RANK, MLA_HEADS * LANES).astype(BF16)
        wukvvt = ukv[:, :, MLA_NOPE:].reshape(MLA_KV_RANK, MLA_HEADS * MLA_V).T.astype(BF16)
        gq = mla_q_norm_w[l].reshape(1, MLA_Q_RANK)
        gkv = mla_kv_norm_w[l].reshape(1, MLA_KV_RANK)
        mod_tok = mod_tok_all[l]

        qa, ka, qn, kn, vn, qm, km, vt = _inproj(xs, mod_tok, tabs, (w_ext, wvt, gq, gkv, wuq, wukvk, wukvvt))

        lam_init = 0.8 - 0.6 * math.exp(-0.3 * l)
        lamp = jnp.zeros((1, 8, DA_QK), F32)
        lamp = lamp.at[0, 0].set(lam_q1[l]).at[0, 1].set(lam_k1[l]).at[0, 2].set(lam_q2[l]).at[0, 3].set(lam_k2[l])
        lamp = lamp.at[0, 4].set(lam_init)
        dnw = jnp.concatenate([diff_norm_w[l], diff_norm_w[l]]).reshape(1, 1, LANES)
        mix_a = _diff_attn(qa, ka, vt, lamp, dnw)

        mix_b = _na_attn(qn, kn, vn, bias_blocks[l:l + 1], hmask, C)
        mix_c = _mla_attn(qm, km, vt)

        cw = jnp.concatenate([conv_w[l], conv_b[l][None], jnp.zeros((4, 2 * D_FF), F32)], axis=0)
        xs = _outproj_ffn(xs, (mix_a, mix_b, mix_c), mod_tok, w_out[l].astype(BF16),
                          ln1_g[l].reshape(1, D), ln1_b[l].reshape(1, D),
                          w_up[l].astype(BF16), cw, w_down[l].astype(BF16),
                          ln2_g[l].reshape(1, D), ln2_b[l].reshape(1, D), alpha, C, latent_only=(l == L - 1))
    return xs
```

```python
import functools
import math

import numpy as np
import jax
import jax.numpy as jnp
from jax import lax
from jax.experimental import pallas as pl
from jax.experimental.pallas import tpu as pltpu

F32 = jnp.float32
BF16 = jnp.bfloat16

GRID_W = 64
ROPE_DIM = 32
ROPE_BASE = 10000.0
DA_HEADS, DA_QK, DA_V = 6, 32, 64
NA_HEADS, NA_DIM, NA_KH, NA_KW = 6, 64, 8, 16
MLA_HEADS, MLA_Q_RANK, MLA_KV_RANK, MLA_NOPE, MLA_ROPE, MLA_V = 4, 256, 128, 64, 32, 64
D_FF = 2816
LN_EPS = 1e-6
NEG_INF = -1e30
DA_SCALE = DA_QK ** -0.5
NA_SCALE = NA_DIM ** -0.5
MLA_SCALE = (MLA_NOPE + MLA_ROPE) ** -0.5
LOG2E = math.log2(math.e)

T = 256
FLASH_UNROLL = 32
MXU_ORDER = (("qk", 0), ("pv", 0), ("qk", 2), ("pv", 2), ("qk", 1), ("pv", 1), ("qk", 3), ("pv", 3))
LANES = 128
LOW_COLS = MLA_Q_RANK + MLA_KV_RANK + 2 * LANES
VT_ROWS = 64
FF_SPLITS = (1024, 1024, 768)
NA_WIN_ROWS = 12
NA_BLOCKS = 2 * NA_KH
HALO = 8
MIX_HALO = 16
VMEM_LIMIT = 56 * 1024 * 1024

_NT = (((1,), (1,)), ((), ()))


def _layernorm(x):
    mu = jnp.mean(x, axis=-1, keepdims=True)
    xc = x - mu
    var = jnp.mean(xc * xc, axis=-1, keepdims=True)
    return xc * lax.rsqrt(var + LN_EPS)


def _rms(x, g):
    return (x * lax.rsqrt(jnp.mean(x * x, axis=-1, keepdims=True) + LN_EPS)) * g


def _cparams(sem):
    return pltpu.CompilerParams(dimension_semantics=sem, vmem_limit_bytes=VMEM_LIMIT)


def _ada_kernel(cond_ref, w_ref, b_ref, o_ref):
    c = cond_ref[...]
    a = (c * jax.nn.sigmoid(c)).astype(BF16)
    o_ref[0] = jnp.dot(a, w_ref[0].astype(BF16), preferred_element_type=F32) + b_ref[0]


def _ada(cond8, w_ada, b_ada):
    L, D, N = w_ada.shape
    tn = 1536
    return pl.pallas_call(
        _ada_kernel,
        grid=(L, N // tn),
        in_specs=[pl.BlockSpec((8, D), lambda l, j: (0, 0)),
                  pl.BlockSpec((1, D, tn), lambda l, j: (l, 0, j)),
                  pl.BlockSpec((1, 1, tn), lambda l, j: (l, 0, j))],
        out_specs=pl.BlockSpec((1, 8, tn), lambda l, j: (l, 0, j)),
        out_shape=jax.ShapeDtypeStruct((L, 8, N), F32),
        compiler_params=_cparams(("arbitrary", "arbitrary")),
        name="ada",
    )(cond8, w_ada, b_ada.reshape(L, 1, N))


def _inproj_kernel(x_ref, mod_ref, cosa_ref, sina_ref, cosm_ref, sinm_ref, w_ref, wvt_ref,
                   gq_ref, gkv_ref, wuq_ref, wukvk_ref, wukvvt_ref,
                   qa_ref, ka_ref, qn_ref, kn_ref, vn_ref, qm_ref, km_ref, vt_ref):
    D = x_ref.shape[-1]
    x = x_ref[0]
    mod = mod_ref[0, 0]
    h = (_layernorm(x) * (1.0 + mod[:, D:2 * D]) + mod[:, 0:D]).astype(BF16)
    lo = jnp.dot(h, w_ref[:, 0:LOW_COLS], preferred_element_type=F32)
    vta = lax.dot_general(wvt_ref[...], h, _NT, preferred_element_type=F32)
    hi = jnp.dot(h, w_ref[:, LOW_COLS:], preferred_element_type=F32)
    proj = jnp.concatenate([hi, lo], axis=1)

    cosa, sina = cosa_ref[...], sina_ref[...]
    cos3 = jnp.concatenate([cosa] * 3, axis=1)
    sin3 = jnp.concatenate([sina] * 3, axis=1)
    qa_ref[0] = ((proj[:, 0:384] * cos3 + proj[:, 384:768] * sin3) * (DA_SCALE * LOG2E)).astype(BF16)
    ka_ref[0] = (proj[:, 768:1152] * cos3 + proj[:, 1152:1536] * sin3).astype(BF16)
    qn_ref[0] = (proj[:, 1536:1920] * (NA_SCALE * LOG2E)).astype(BF16)
    kn_ref[0] = proj[:, 1920:2304].astype(BF16)
    vn_ref[0] = proj[:, 2304:2688].astype(BF16)

    cosm, sinm = cosm_ref[...], sinm_ref[...]
    cos4 = jnp.concatenate([cosm] * 4, axis=1)
    sin4 = jnp.concatenate([sinm] * 4, axis=1)
    cqn = _rms(proj[:, 2688:2944], gq_ref[...]).astype(BF16)
    qc = jnp.dot(cqn, wuq_ref[...], preferred_element_type=F32)
    qm_ref[0] = ((qc[:, 0:512] * cos4 + qc[:, 512:1024] * sin4) * (MLA_SCALE * LOG2E)).astype(BF16)
    kvn = _rms(proj[:, 2944:3072], gkv_ref[...]).astype(BF16)
    kr = proj[:, 3072:3200] * cosm + proj[:, 3200:3328] * sinm
    km = jnp.dot(kvn, wukvk_ref[...], preferred_element_type=F32) + jnp.concatenate([kr] * 4, axis=1)
    km_ref[0] = km.astype(BF16)

    vtm = lax.dot_general(wukvvt_ref[...], kvn, _NT, preferred_element_type=F32)
    for hd in range(MLA_HEADS):
        vt_ref[0, hd, 0] = vtm[hd * MLA_V:(hd + 1) * MLA_V].astype(BF16)
    for hd in range(DA_HEADS):
        vt_ref[0, MLA_HEADS + hd, 0] = vta[hd * DA_V:(hd + 1) * DA_V].astype(BF16)


def _inproj(xs, mod_tok, tabs, wts):
    B, S, D = xs.shape
    nt = S // T
    cosa, sina, cosm, sinm = tabs
    w_ext, wvt, gq, gkv, wuq, wukvk, wukvvt = wts
    tok = lambda w: pl.BlockSpec((1, T, w), lambda b, i: (b, i, 0))
    tab = pl.BlockSpec((T, LANES), lambda b, i: (i, 0))
    full = lambda a: pl.BlockSpec(a.shape, lambda b, i: (0,) * a.ndim)
    nvt = DA_HEADS + MLA_HEADS
    out_shape = [jax.ShapeDtypeStruct((B, S, w), BF16) for w in (384, 384, 384, 384, 384, 512, 512)]
    out_shape.append(jax.ShapeDtypeStruct((B, nvt, nt, VT_ROWS, T), BF16))
    out_specs = [tok(w) for w in (384, 384, 384, 384, 384, 512, 512)]
    out_specs.append(pl.BlockSpec((1, nvt, 1, VT_ROWS, T), lambda b, i: (b, 0, i, 0, 0)))
    return pl.pallas_call(
        _inproj_kernel,
        grid=(B, nt),
        in_specs=[tok(D),
                  pl.BlockSpec((1, 1, 1, 6 * D), lambda b, i: (b, jnp.minimum(i, 1), 0, 0)),
                  tab, tab, tab, tab,
                  full(w_ext), full(wvt), full(gq), full(gkv), full(wuq), full(wukvk), full(wukvvt)],
        out_specs=out_specs,
        out_shape=out_shape,
        compiler_params=_cparams(("arbitrary", "arbitrary")),
        name="inproj",
    )(xs, mod_tok, cosa, sina, cosm, sinm, w_ext, wvt, gq, gkv, wuq, wukvk, wukvvt)


def _flash_t(qts, k_rows_fn, vt_tile_fn, ctx_only, nchunks, scratch, finish):
    p_sc, m_sc, l_sc, acc_sc = scratch
    nsub = len(qts)

    def cols(s):
        return slice(s * T, (s + 1) * T)

    def chunk_start(j):
        return j * T if isinstance(j, int) else pl.multiple_of(j * T, T)

    def scores(j):
        return [jnp.dot(k_rows_fn(chunk_start(j), T, s), qts[s], preferred_element_type=F32) for s in range(nsub)]

    def sublane_sums(p):
        return jnp.sum(p.reshape(T // 8, 8, p.shape[1]), axis=0)

    def normalised():
        acc = acc_sc[...]
        l = jnp.sum(l_sc[...], axis=0, keepdims=True)
        return [acc[:, cols(s)] / l[:, cols(s)] for s in range(nsub)]

    def attend(nch):
        st0 = scores(0)
        for s in range(nsub):
            m0 = jnp.max(st0[s], axis=0, keepdims=True)
            p0 = jnp.exp2(st0[s] - m0)
            m_sc[:, cols(s)] = m0
            l_sc[:, cols(s)] = sublane_sums(p0)
            p_sc[:, cols(s)] = p0.astype(BF16)
        acc_sc[...] = jnp.zeros(acc_sc.shape, F32)

        unroll = math.gcd(nch - 1, FLASH_UNROLL) if nch > 1 else 1

        def body(i, carry):
            pvs, ls, p_prev = [None] * nsub, [None] * nsub, None
            for c in range(unroll):
                j = 1 + i * unroll + c
                st = [None] * nsub
                for kind, s in MXU_ORDER:
                    if kind == "qk":
                        st[s] = jnp.dot(k_rows_fn(chunk_start(j), T, s), qts[s], preferred_element_type=F32)
                    else:
                        p = p_sc[:, cols(s)] if c == 0 else p_prev[s]
                        d = jnp.dot(vt_tile_fn(j - 1, s), p, preferred_element_type=F32)
                        pvs[s] = d if c == 0 else pvs[s] + d
                p_prev = []
                for s in range(nsub):
                    p = jnp.exp2(st[s] - m_sc[:, cols(s)])
                    ls[s] = sublane_sums(p) if c == 0 else ls[s] + sublane_sums(p)
                    p_prev.append(p.astype(BF16))
            for s in range(nsub):
                p_sc[:, cols(s)] = p_prev[s]
                l_sc[:, cols(s)] += ls[s]
                acc_sc[:, cols(s)] += pvs[s]
            return carry

        ntrips = (nch - 1) // unroll
        if ntrips == 1:
            body(0, 0)
        elif ntrips > 1:
            lax.fori_loop(0, ntrips, body, 0)
        for s in range(nsub):
            acc_sc[:, cols(s)] += jnp.dot(vt_tile_fn(nch - 1, s), p_sc[:, cols(s)], preferred_element_type=F32)

        finite = jnp.logical_and(jnp.isfinite(jnp.sum(acc_sc[...], axis=0, keepdims=True)),
                                 jnp.isfinite(jnp.sum(l_sc[...], axis=0, keepdims=True)))
        overflowed = jnp.max(jnp.where(finite, 0.0, 1.0)) > 0.5
        finish(normalised())

        @pl.when(overflowed)
        def _():
            m_sc[...] = jnp.full(m_sc.shape, NEG_INF, F32)
            l_sc[...] = jnp.zeros(l_sc.shape, F32)
            acc_sc[...] = jnp.zeros(acc_sc.shape, F32)

            def exact(j, carry):
                st = scores(j)
                for s in range(nsub):
                    m_old = m_sc[:, cols(s)]
                    m_new = jnp.maximum(m_old, jnp.max(st[s], axis=0, keepdims=True))
                    alpha = jnp.exp2(m_old - m_new)
                    p = jnp.exp2(st[s] - m_new)
                    pv = jnp.dot(vt_tile_fn(j, s), p.astype(BF16), preferred_element_type=F32)
                    acc_sc[:, cols(s)] = alpha * acc_sc[:, cols(s)] + pv
                    l_sc[:, cols(s)] = alpha * l_sc[:, cols(s)] + sublane_sums(p)
                    m_sc[:, cols(s)] = m_new
                return carry

            lax.fori_loop(0, nch, exact, 0)
            finish(normalised())

    pl.when(ctx_only)(lambda: attend(1))
    pl.when(jnp.logical_not(ctx_only))(lambda: attend(nchunks))


def _flash_scratch(nsub):
    w = nsub * T
    return [pltpu.VMEM((T, w), BF16),
            pltpu.VMEM((1, w), F32),
            pltpu.VMEM((8, w), F32),
            pltpu.VMEM((VT_ROWS, w), F32)]


def _diff_kernel(q_ref, k_ref, vt_ref, lamp_ref, dnw_ref, o_ref, *scratch):
    ctx_only = pl.program_id(2) == 0
    nchunks = k_ref.shape[1] // T
    qf = q_ref[0].astype(F32)
    qt = qf.T
    grp = lax.broadcasted_iota(jnp.int32, qt.shape, 0) // DA_QK
    qts = [jnp.where(grp == s, qt, 0.0).astype(BF16) for s in range(4)]

    def k_rows(start, n, s):
        return k_ref[0, pl.ds(start, n), :]

    def vt_tile(j, s):
        return vt_ref[0, s // 2, j]

    def finish(outs):
        lp = lamp_ref[0]
        lam_init = lp[4:5, 0:1]
        lam = (jnp.exp(jnp.sum(lp[0:1] * lp[1:2], axis=1, keepdims=True))
               - jnp.exp(jnp.sum(lp[2:3] * lp[3:4], axis=1, keepdims=True)) + lam_init)
        ys = []
        for hd in range(2):
            o = outs[2 * hd] - lam * outs[2 * hd + 1]
            ys.append(o * lax.rsqrt(jnp.mean(o * o, axis=0, keepdims=True) + LN_EPS))
        y = jnp.concatenate(ys, axis=0).T
        o_ref[0] = ((y * dnw_ref[0]) * (1.0 - lam_init)).astype(o_ref.dtype)

    _flash_t(qts, k_rows, vt_tile, ctx_only, nchunks, scratch, finish)


def _diff_attn(qa, ka, vt, lamp_l, dnw_l):
    B, S, _ = qa.shape
    nt = S // T
    G = DA_HEADS // 2
    return pl.pallas_call(
        _diff_kernel,
        grid=(B, G, nt),
        in_specs=[pl.BlockSpec((1, T, LANES), lambda b, g, i: (b, i, g)),
                  pl.BlockSpec((1, S, LANES), lambda b, g, i: (b, 0, g)),
                  pl.BlockSpec((1, 2, nt, VT_ROWS, T), lambda b, g, i: (b, MLA_HEADS // 2 + g, 0, 0, 0)),
                  pl.BlockSpec((1, 8, DA_QK), lambda b, g, i: (0, 0, 0)),
                  pl.BlockSpec((1, 1, LANES), lambda b, g, i: (0, 0, 0))],
        out_specs=pl.BlockSpec((1, T, LANES), lambda b, g, i: (b, i, g)),
        out_shape=jax.ShapeDtypeStruct((B, S, DA_HEADS * DA_V), BF16),
        scratch_shapes=_flash_scratch(4),
        compiler_params=_cparams(("arbitrary", "arbitrary", "arbitrary")),
        name="diff_attn",
    )(qa, ka, vt, lamp_l, dnw_l)


def _mla_kernel(q_ref, k_ref, vt_ref, o_ref, *scratch):
    ctx_only = pl.program_id(1) == 0
    nchunks = k_ref.shape[1] // T
    qf = q_ref[0].astype(F32)
    qts = [qf[:, s * LANES:(s + 1) * LANES].T.astype(BF16) for s in range(MLA_HEADS)]

    def k_rows(start, n, s):
        return k_ref[0, pl.ds(start, n), s * LANES:(s + 1) * LANES]

    def vt_tile(j, s):
        return vt_ref[0, s, j]

    def finish(outs):
        o_ref[0] = jnp.concatenate(outs, axis=0).T.astype(o_ref.dtype)

    _flash_t(qts, k_rows, vt_tile, ctx_only, nchunks, scratch, finish)


def _mla_attn(qm, km, vt):
    B, S, W = qm.shape
    nt = S // T
    return pl.pallas_call(
        _mla_kernel,
        grid=(B, nt),
        in_specs=[pl.BlockSpec((1, T, W), lambda b, i: (b, i, 0)),
                  pl.BlockSpec((1, S, W), lambda b, i: (b, 0, 0)),
                  pl.BlockSpec((1, MLA_HEADS, nt, VT_ROWS, T), lambda b, i: (b, 0, 0, 0, 0))],
        out_specs=pl.BlockSpec((1, T, MLA_HEADS * MLA_V), lambda b, i: (b, i, 0)),
        out_shape=jax.ShapeDtypeStruct((B, S, MLA_HEADS * MLA_V), BF16),
        scratch_shapes=_flash_scratch(MLA_HEADS),
        compiler_params=_cparams(("arbitrary", "arbitrary")),
        name="mla_attn",
    )(qm, km, vt)


def _na_kernel(q_ref, k_ref, v_ref, blk_ref, hmask_ref, o_ref, bias_sc, *, ctx_len, rows, nlat):
    i = pl.program_id(2)
    qf = q_ref[0].astype(F32)
    lane_head = hmask_ref[...]
    k_ctx = k_ref[0, 0:ctx_len, :]
    v_ctx = v_ref[0, 0:ctx_len, :]

    @pl.when(i == 0)
    def _():
        blk = _na_block_index(rows)
        left = lax.broadcasted_iota(jnp.int32, (GRID_W, LANES), 1) < GRID_W
        for hd in range(2):
            for v in range(3):
                for a in range(T // GRID_W):
                    for kp in range(NA_WIN_ROWS // 2):
                        tile = jnp.where(left, blk_ref[0, hd * NA_BLOCKS + int(blk[v, a, 2 * kp])],
                                         blk_ref[0, hd * NA_BLOCKS + int(blk[v, a, 2 * kp + 1])])
                        bias_sc[v, hd, a * GRID_W:(a + 1) * GRID_W, kp * LANES:(kp + 1) * LANES] = tile

        out = jnp.zeros((T, LANES), F32)
        for hd in range(2):
            hm = lane_head[hd:hd + 1, :]
            qh = (qf * hm).astype(BF16)
            s = lax.dot_general(qh, k_ctx, _NT, preferred_element_type=F32)
            m = jnp.max(s, axis=1, keepdims=True)
            p = jnp.exp2(s - m)
            l = jnp.sum(p, axis=1, keepdims=True)
            o = jnp.dot(p.astype(BF16), v_ctx, preferred_element_type=F32) / l
            out = out + o * hm
        o_ref[0] = out.astype(o_ref.dtype)

    @pl.when(i > 0)
    def _():
        r = (i - 1) * (T // GRID_W)
        start = jnp.clip(r - NA_KH // 2, 0, rows - NA_WIN_ROWS)
        off = pl.multiple_of(ctx_len + start * GRID_W, GRID_W)
        k_win = k_ref[0, pl.ds(off, NA_WIN_ROWS * GRID_W), :]
        v_win = v_ref[0, pl.ds(off, NA_WIN_ROWS * GRID_W), :]
        variant = jnp.where(i == 1, 0, jnp.where(i == nlat, 2, 1))
        out = jnp.zeros((T, LANES), F32)
        for hd in range(2):
            hm = lane_head[hd:hd + 1, :]
            qh = (qf * hm).astype(BF16)
            s_w = lax.dot_general(qh, k_win, _NT, preferred_element_type=F32) + bias_sc[variant, hd]
            s_c = lax.dot_general(qh, k_ctx, _NT, preferred_element_type=F32)
            m = jnp.maximum(jnp.max(s_w, axis=1, keepdims=True), jnp.max(s_c, axis=1, keepdims=True))
            p_w = jnp.exp2(s_w - m)
            p_c = jnp.exp2(s_c - m)
            l = jnp.sum(p_w, axis=1, keepdims=True) + jnp.sum(p_c, axis=1, keepdims=True)
            o = (jnp.dot(p_w.astype(BF16), v_win, preferred_element_type=F32)
                 + jnp.dot(p_c.astype(BF16), v_ctx, preferred_element_type=F32)) / l
            out = out + o * hm
        o_ref[0] = out.astype(o_ref.dtype)


def _na_attn(qn, kn, vn, blocks_l, hmask, ctx_len):
    B, S, _ = qn.shape
    nt = S // T
    rows = (S - ctx_len) // GRID_W
    G = NA_HEADS // 2
    nlat = nt - ctx_len // T
    return pl.pallas_call(
        functools.partial(_na_kernel, ctx_len=ctx_len, rows=rows, nlat=nlat),
        grid=(B, G, nt),
        in_specs=[pl.BlockSpec((1, T, LANES), lambda b, g, i: (b, i, g)),
                  pl.BlockSpec((1, S, LANES), lambda b, g, i: (b, 0, g)),
                  pl.BlockSpec((1, S, LANES), lambda b, g, i: (b, 0, g)),
                  pl.BlockSpec((1, 2 * NA_BLOCKS, GRID_W, LANES), lambda b, g, i: (0, g, 0, 0)),
                  pl.BlockSpec((8, LANES), lambda b, g, i: (0, 0))],
        out_specs=pl.BlockSpec((1, T, LANES), lambda b, g, i: (b, i, g)),
        out_shape=jax.ShapeDtypeStruct((B, S, NA_HEADS * NA_DIM), BF16),
        scratch_shapes=[pltpu.VMEM((3, 2, T, NA_WIN_ROWS * GRID_W), F32)],
        compiler_params=_cparams(("arbitrary", "arbitrary", "arbitrary")),
        name="na_attn",
    )(qn, kn, vn, blocks_l, hmask)


def _ffn_kernel(x_ref, xp_ref, xn_ref, ma_ref, mb_ref, mc_ref, map_ref, mbp_ref, mcp_ref, man_ref, mbn_ref, mcn_ref,
                mod_ref, wo_ref, g1_ref, b1_ref, wu_ref, cw_ref, wd_ref, g_ref, b_ref,
                o_ref, *u_refs, alpha, nlat_first, nlast):
    D = x_ref.shape[-1]
    i = pl.program_id(1)
    mod = mod_ref[0, 0]
    gate1 = mod[:, 2 * D:3 * D]
    shift, scale, gate = mod[:, 3 * D:4 * D], mod[:, 4 * D:5 * D], mod[:, 5 * D:6 * D]
    mix = jnp.concatenate([jnp.concatenate([a[0], b[0], c[0]], axis=1)
                           for a, b, c in ((map_ref, mbp_ref, mcp_ref), (ma_ref, mb_ref, mc_ref),
                                           (man_ref, mbn_ref, mcn_ref))], axis=0)
    xe = jnp.concatenate([xp_ref[0], x_ref[0], xn_ref[0]], axis=0)
    y1 = jnp.dot(mix, wo_ref[...], preferred_element_type=F32)
    x1e = _layernorm(alpha * xe + gate1 * y1) * g1_ref[...] + b1_ref[...]
    x = x1e[MIX_HALO:MIX_HALO + T]
    has_prev = jnp.logical_and(i != 0, i != nlat_first)
    has_next = jnp.logical_and(i != nlat_first - 1, i != nlast)
    hp = (_layernorm(x1e[MIX_HALO - HALO:MIX_HALO]) * (1.0 + scale) + shift) * jnp.where(has_prev, 1.0, 0.0)
    hn = (_layernorm(x1e[MIX_HALO + T:MIX_HALO + T + HALO]) * (1.0 + scale) + shift) * jnp.where(has_next, 1.0, 0.0)
    hc = _layernorm(x) * (1.0 + scale) + shift
    h = jnp.concatenate([hp, hc, hn], axis=0).astype(BF16)
    n = x.shape[0]
    dff = wd_ref.shape[0]
    nck = len(FF_SPLITS)
    edges = [sum(FF_SPLITS[:j]) for j in range(nck + 1)]
    gcols = [slice(edges[j], edges[j + 1]) for j in range(nck)]
    vcols = [slice(dff + edges[j], dff + edges[j + 1]) for j in range(nck)]

    def up(j):
        u_refs[2 * j][...] = jnp.dot(h, wu_ref[:, gcols[j]], preferred_element_type=F32)
        u_refs[2 * j + 1][...] = jnp.dot(h, wu_ref[:, vcols[j]], preferred_element_type=F32)

    def conv(u_ref, c):
        return (u_ref[HALO - 1:HALO - 1 + n, :] * c[0:1] + u_ref[HALO:HALO + n, :] * c[1:2]
                + u_ref[HALO + 1:HALO + 1 + n, :] * c[2:3] + c[3:4])

    def down(j):
        ug = conv(u_refs[2 * j], cw_ref[:, gcols[j]])
        uv = conv(u_refs[2 * j + 1], cw_ref[:, vcols[j]])
        a = ((ug * jax.nn.sigmoid(ug)) * uv).astype(BF16)
        return jnp.dot(a, wd_ref[gcols[j], :], preferred_element_type=F32)

    up(0)
    y = None
    for j in range(nck):
        if j + 1 < nck:
            up(j + 1)
        d = down(j)
        y = d if y is None else y + d
    o_ref[0] = _layernorm(alpha * x + gate * y) * g_ref[...] + b_ref[...]


def _outproj_ffn(xs, mixes, mod_tok, wo, g1, b1, wu, cw, wd, g2, b2, alpha, ctx_len, latent_only):
    B, S, D = xs.shape
    nt = S // T
    th = T // MIX_HALO
    nh = S // MIX_HALO
    nctx = ctx_len // T
    tok = pl.BlockSpec((1, T, D), lambda b, i: (b, i, 0))
    full = lambda a: pl.BlockSpec(a.shape, lambda b, i: (0,) * a.ndim)
    tokw = lambda w: pl.BlockSpec((1, T, w), lambda b, i: (b, i, 0))
    prevw = lambda w: pl.BlockSpec((1, MIX_HALO, w), lambda b, i: (b, jnp.maximum(i * th - 1, 0), 0))
    nextw = lambda w: pl.BlockSpec((1, MIX_HALO, w), lambda b, i: (b, jnp.minimum((i + 1) * th, nh - 1), 0))
    widths = [m.shape[-1] for m in mixes]
    if latent_only:
        out_spec = pl.BlockSpec((1, T, D), lambda b, i: (b, jnp.maximum(i - nctx, 0), 0))
        out_rows = S - ctx_len
    else:
        out_spec, out_rows = tok, S
    return pl.pallas_call(
        functools.partial(_ffn_kernel, alpha=alpha, nlat_first=ctx_len // T, nlast=nt - 1),
        grid=(B, nt),
        in_specs=[tok, prevw(D), nextw(D),
                  *[tokw(w) for w in widths], *[prevw(w) for w in widths], *[nextw(w) for w in widths],
                  pl.BlockSpec((1, 1, 1, 6 * D), lambda b, i: (b, jnp.minimum(i, 1), 0, 0)),
                  full(wo), full(g1), full(b1), full(wu), full(cw), full(wd), full(g2), full(b2)],
        out_specs=out_spec,
        out_shape=jax.ShapeDtypeStruct((B, out_rows, D), F32),
        scratch_shapes=[pltpu.VMEM((T + 2 * HALO, w), F32) for w in FF_SPLITS for _ in range(2)],
        compiler_params=_cparams(("arbitrary", "arbitrary")),
        name="outproj_ffn",
    )(xs, xs, xs, *mixes, *mixes, *mixes, mod_tok, wo, g1, b1, wu, cw, wd, g2, b2)


def _rot_perm(width):
    j = np.arange(width)
    jj = j % (ROPE_DIM // 2)
    lo = jj < ROPE_DIM // 4
    src = np.where(lo, j + ROPE_DIM // 4, j - ROPE_DIM // 4)
    sign = np.where(lo, -1.0, 1.0).astype(np.float32)
    return src, sign


def _rope_tables(n, ctx_len):
    t = jnp.arange(n)
    row = (t // GRID_W).astype(F32)
    col = (t % GRID_W).astype(F32)
    axis_dim = ROPE_DIM // 2
    inv_freq = ROPE_BASE ** (-jnp.arange(0, axis_dim, 2, dtype=F32) / axis_dim)

    def cs(pos):
        ang = pos[:, None] * inv_freq[None, :]
        ang = jnp.concatenate([ang, ang], axis=-1)
        return jnp.cos(ang), jnp.sin(ang)

    cr, sr = cs(row)
    cc, sc = cs(col)
    cos32 = jnp.concatenate([cr, cc], axis=-1)
    sin32 = jnp.concatenate([sr, sc], axis=-1)
    cos32 = jnp.concatenate([jnp.ones((ctx_len, ROPE_DIM), F32), cos32], axis=0)
    sin32 = jnp.concatenate([jnp.zeros((ctx_len, ROPE_DIM), F32), sin32], axis=0)
    S = n + ctx_len
    cosa = jnp.tile(cos32, (1, LANES // ROPE_DIM))
    sina = jnp.tile(sin32, (1, LANES // ROPE_DIM))
    cosm = jnp.concatenate([jnp.ones((S, MLA_NOPE), F32), cos32, jnp.ones((S, LANES - MLA_NOPE - MLA_ROPE), F32)], axis=1)
    sinm = jnp.concatenate([jnp.zeros((S, MLA_NOPE), F32), sin32, jnp.zeros((S, LANES - MLA_NOPE - MLA_ROPE), F32)], axis=1)
    return cosa, sina, cosm, sinm


def _na_block_index(rows):
    qrows = T // GRID_W
    masked = 2 * NA_KH - 1
    blk = np.full((3, qrows, NA_WIN_ROWS), masked, np.int32)
    for v, r in enumerate((0, 2 * qrows, rows - qrows)):
        start = int(np.clip(r - NA_KH // 2, 0, rows - NA_WIN_ROWS))
        for a in range(qrows):
            qr = r + a
            r0 = int(np.clip(qr - NA_KH // 2, 0, rows - NA_KH))
            for kb in range(NA_WIN_ROWS):
                kr = start + kb
                if r0 <= kr < r0 + NA_KH:
                    blk[v, a, kb] = kr - qr + (NA_KH - 1)
    return blk


def _bias_block_kernel(rpb_ref, dc_ref, band_ref, o_ref, *, nh, ndr, ndc):
    l = pl.program_id(0)
    dc = dc_ref[...]
    band = band_ref[...] > 0
    neg = jnp.full(dc.shape, NEG_INF, F32)
    for h in range(nh):
        def one(r, carry, h=h):
            row = (l * nh + h) * ndr + r
            tile = neg
            for d in range(ndc):
                tile = jnp.where(dc == d, rpb_ref[row, d] * LOG2E, tile)
            o_ref[0, h * NA_BLOCKS + r] = jnp.where(band, tile, NEG_INF)
            return carry
        lax.fori_loop(0, ndr, one, 0)
        o_ref[0, h * NA_BLOCKS + NA_BLOCKS - 1] = neg


def _na_bias_blocks(na_rpb):
    L, H, ndr, ndc = na_rpb.shape
    assert ndr == NA_BLOCKS - 1
    qc = np.arange(GRID_W)
    c0 = np.clip(qc - NA_KW // 2, 0, GRID_W - NA_KW)
    band = (qc[None, :] >= c0[:, None]) & (qc[None, :] < c0[:, None] + NA_KW)
    dcm = np.clip(qc[None, :] - qc[:, None], -(NA_KW - 1), NA_KW - 1) + (NA_KW - 1)
    dup = lambda a: jnp.asarray(np.concatenate([a, a], axis=1), jnp.int32)
    return pl.pallas_call(
        functools.partial(_bias_block_kernel, nh=H, ndr=ndr, ndc=ndc),
        grid=(L,),
        in_specs=[pl.BlockSpec(memory_space=pltpu.SMEM),
                  pl.BlockSpec((GRID_W, LANES), lambda l: (0, 0)),
                  pl.BlockSpec((GRID_W, LANES), lambda l: (0, 0))],
        out_specs=pl.BlockSpec((1, H * NA_BLOCKS, GRID_W, LANES), lambda l: (l, 0, 0, 0)),
        out_shape=jax.ShapeDtypeStruct((L, H * NA_BLOCKS, GRID_W, LANES), F32),
        compiler_params=_cparams(("arbitrary",)),
        name="na_bias_blocks",
    )(na_rpb.reshape(L * H * ndr, ndc), dup(dcm), dup(band))


def kernel(x, c, ctx, c_ctx, w_ada, b_ada, w_in, lam_q1, lam_k1, lam_q2, lam_k2, diff_norm_w, na_rpb,
           mla_q_norm_w, mla_kv_norm_w, w_uq, w_ukv, w_out, ln1_g, ln1_b, w_up, conv_w, conv_b, w_down,
           ln2_g, ln2_b):
    B, N, D = x.shape
    C = ctx.shape[1]
    L = w_in.shape[0]
    S = C + N
    rows = N // GRID_W
    assert C == T and N % T == 0
    assert rows >= NA_WIN_ROWS and B + 1 <= 8 and sum(FF_SPLITS) == D_FF
    alpha = (2 * L) ** 0.25

    cond8 = jnp.zeros((8, D), F32).at[0:B].set(c).at[B].set(c_ctx)
    mod_all = _ada(cond8, w_ada, b_ada)
    ctx_rows = jnp.broadcast_to(mod_all[:, B][:, None], (L, B, 6 * D))
    mod_tok_all = jnp.stack([ctx_rows, mod_all[:, 0:B]], axis=2).reshape(L, B, 2, 1, 6 * D)

    tabs = _rope_tables(N, C)
    lane = np.arange(LANES)
    hmask = np.zeros((8, LANES), np.float32)
    for hd in range(2):
        hmask[hd] = (lane // NA_DIM == hd)
    hmask = jnp.asarray(hmask)
    bias_blocks = _na_bias_blocks(na_rpb)

    src_a, sgn_a = _rot_perm(DA_HEADS * 2 * DA_QK)
    src_r, sgn_r = _rot_perm(MLA_ROPE)
    sp = np.cumsum([0, 384, 384, 384, 384, 384, 384, MLA_Q_RANK, MLA_KV_RANK, MLA_ROPE])
    zeros = lambda *s: jnp.zeros(s, F32)

    xs = jnp.concatenate([ctx, x], axis=1)
    for l in range(L):
        wi = w_in[l]
        a_q, a_k, a_v, n_q, n_k, n_v, c_q, c_kv, k_r = [wi[:, sp[t]:sp[t + 1]] for t in range(9)]
        kr_pad = jnp.concatenate([zeros(D, MLA_NOPE), k_r, zeros(D, LANES - MLA_NOPE - MLA_ROPE)], axis=1)
        krr_pad = jnp.concatenate([zeros(D, MLA_NOPE), k_r[:, src_r] * sgn_r,
                                   zeros(D, LANES - MLA_NOPE - MLA_ROPE)], axis=1)
        w_ext = jnp.concatenate([c_q, c_kv, kr_pad, krr_pad,
                                 a_q, a_q[:, src_a] * sgn_a, a_k, a_k[:, src_a] * sgn_a, n_q, n_k, n_v],
                                axis=1).astype(BF16)
        wvt = a_v.T.astype(BF16)
        uq = w_uq[l].reshape(MLA_Q_RANK, MLA_HEADS, MLA_NOPE + MLA_ROPE)
        uq_rope = uq[:, :, MLA_NOPE:]
        padq = jnp.zeros((MLA_Q_RANK, MLA_HEADS, LANES - MLA_NOPE - MLA_ROPE), F32)
        wuq = jnp.concatenate([
            jnp.concatenate([uq, padq], axis=2).reshape(MLA_Q_RANK, MLA_HEADS * LANES),
            jnp.concatenate([jnp.zeros_like(uq[:, :, :MLA_NOPE]), uq_rope[:, :, src_r] * sgn_r, padq],
                            axis=2).reshape(MLA_Q_RANK, MLA_HEADS * LANES)], axis=1).astype(BF16)
        ukv = w_ukv[l].reshape(MLA_KV_RANK, MLA_HEADS, MLA_NOPE + MLA_V)
        wukvk = jnp.concatenate([ukv[:, :, :MLA_NOPE], jnp.zeros((MLA_KV_RANK, MLA_HEADS, LANES - MLA_NOPE), F32)],
                                axis=2).reshape(MLA_KV_RANK, MLA_HEADS * LANES).astype(BF16)
        wukvvt = ukv[:, :, MLA_NOPE:].reshape(MLA_KV_RANK, MLA_HEADS * MLA_V).T.astype(BF16)
        gq = mla_q_norm_w[l].reshape(1, MLA_Q_RANK)
        gkv = mla_kv_norm_w[l].reshape(1, MLA_KV_RANK)
        mod_tok = mod_tok_all[l]

        qa, ka, qn, kn, vn, qm, km, vt = _inproj(xs, mod_tok, tabs, (w_ext, wvt, gq, gkv, wuq, wukvk, wukvvt))

        lam_init = 0.8 - 0.6 * math.exp(-0.3 * l)
        lamp = jnp.zeros((1, 8, DA_QK), F32)
        lamp = lamp.at[0, 0].set(lam_q1[l]).at[0, 1].set(lam_k1[l]).at[0, 2].set(lam_q2[l]).at[0, 3].set(lam_k2[l])
        lamp = lamp.at[0, 4].set(lam_init)
        dnw = jnp.concatenate([diff_norm_w[l], diff_norm_w[l]]).reshape(1, 1, LANES)
        mix_a = _diff_attn(qa, ka, vt, lamp, dnw)

        mix_b = _na_attn(qn, kn, vn, bias_blocks[l:l + 1], hmask, C)
        mix_c = _mla_attn(qm, km, vt)

        cw = jnp.concatenate([conv_w[l], conv_b[l][None], jnp.zeros((4, 2 * D_FF), F32)], axis=0)
        xs = _outproj_ffn(xs, (mix_a, mix_b, mix_c), mod_tok, w_out[l].astype(BF16),
                          ln1_g[l].reshape(1, D), ln1_b[l].reshape(1, D),
                          w_up[l].astype(BF16), cw, w_down[l].astype(BF16),
                          ln2_g[l].reshape(1, D), ln2_b[l].reshape(1, D), alpha, C, latent_only=(l == L - 1))
    return xs
```
